```python
import math
import jax
import jax.numpy as jnp
from jax import lax
import numpy as np

D_MODEL = 1024
BATCH = 4
SEQ = 4096
DEPTH = 1
DEC_BATCH = 128
DEC_SEQ = 1
PAST_LEN = 8192
PAGE_SIZE = 128

GLA_HEADS = 4
GLA_DK = D_MODEL // (4 * GLA_HEADS)
GLA_DV = D_MODEL // (2 * GLA_HEADS)
GLA_GATE_RANK = 16
GLA_GATE_NORM = 16.0
GLA_CHUNK = 64
DIFF_HEADS = 4
DIFF_HD = D_MODEL // (4 * DIFF_HEADS)
DIFF_VD = 2 * DIFF_HD
X_HEADS = 4
X_HD = D_MODEL // (2 * X_HEADS)
N_MEM = 256
N_EXPERTS = 32
TOP_K = 4
D_FF = D_MODEL
SWIGLU_LIMIT = 7.0
SWIGLU_ALPHA = 1.702
MOE_BLOCK = 128
Q_BLOCK = 128
PAGES_PER_STEP = 8
ROPE_THETA = 10000.0
EPS = 1e-6
IN_SIZES = (GLA_HEADS * GLA_DK, GLA_HEADS * GLA_DK, GLA_HEADS * GLA_DV, GLA_HEADS * GLA_DV,
            GLA_GATE_RANK, DIFF_HEADS * 2 * DIFF_HD, DIFF_HEADS * 2 * DIFF_HD, DIFF_HEADS * DIFF_VD)
IN_WIDTH = sum(IN_SIZES)

kernel_name = 'hybrid_gla_diffattn_moe_step'


def rms_norm(x, g):
    xf = x.astype(jnp.float32)
    y = xf * lax.rsqrt(jnp.mean(xf * xf, axis=-1, keepdims=True) + EPS)
    return y.astype(x.dtype) * g


def rope(x, pos):
    d = x.shape[-1]
    half = d // 2
    inv = jnp.exp(-math.log(ROPE_THETA) * jnp.arange(half, dtype=jnp.float32) * (2.0 / d))
    ang = pos[:, None] * inv[None, :]
    shape = (1, pos.shape[0]) + (1,) * (x.ndim - 3) + (half,)
    cos = jnp.cos(ang).reshape(shape)
    sin = jnp.sin(ang).reshape(shape)
    xf = x.astype(jnp.float32)
    x1, x2 = xf[..., :half], xf[..., half:]
    return jnp.concatenate([x1 * cos - x2 * sin, x2 * cos + x1 * sin], axis=-1).astype(x.dtype)


def split_in(proj):
    idx, acc = [], 0
    for s in IN_SIZES[:-1]:
        acc += s
        idx.append(acc)
    return jnp.split(proj, idx, axis=-1)


def mixer_inputs(u, pos, p):
    b, t, _ = u.shape
    gq, gk, gv, gr, ga, dq, dk, dv = split_in(jnp.dot(u, p['w_in']))
    gla_q = gq.reshape(b, t, GLA_HEADS, GLA_DK) * (GLA_DK ** -0.5)
    gla_k = gk.reshape(b, t, GLA_HEADS, GLA_DK)
    gla_v = gv.reshape(b, t, GLA_HEADS, GLA_DV)
    gla_g = (jax.nn.log_sigmoid((jnp.dot(ga, p['w_gla_a2']) + p['b_gla_a']).astype(jnp.float32))
             / GLA_GATE_NORM).reshape(b, t, GLA_HEADS, GLA_DK)
    q = rope(dq.reshape(b, t, DIFF_HEADS, 2, DIFF_HD), pos)
    k = rope(dk.reshape(b, t, DIFF_HEADS, 2, DIFF_HD), pos)
    v = dv.reshape(b, t, DIFF_HEADS, DIFF_VD)
    return (gla_q, gla_k, gla_v, gla_g, gr), (q, k, v)


def gla_scan(q, k, v, g, s0):
    b, t, h, _ = q.shape
    c = math.gcd(t, GLA_CHUNK)
    nc = t // c

    def to_chunks(a):
        return a.astype(jnp.float32).reshape(b, nc, c, h, a.shape[-1]).transpose(1, 0, 3, 2, 4)

    mask = jnp.tril(jnp.ones((c, c), dtype=bool))

    def step(s, inp):
        qc, kc, vc, gc = inp
        cum = jnp.cumsum(gc, axis=2)
        o_inter = jnp.einsum('bhtk,bhkv->bhtv', qc * jnp.exp(cum), s)
        diff = cum[:, :, :, None, :] - cum[:, :, None, :, :]
        decay = jnp.exp(jnp.where(mask[None, None, :, :, None], diff, -jnp.inf))
        a = jnp.einsum('bhtk,bhsk,bhtsk->bhts', qc, kc, decay)
        o = o_inter + jnp.einsum('bhts,bhsv->bhtv', a, vc)
        last = cum[:, :, -1:, :]
        s_new = jnp.exp(last[:, :, 0, :])[..., None] * s + jnp.einsum(
            'bhsk,bhsv->bhkv', kc * jnp.exp(last - cum), vc)
        return s_new, o

    s_fin, o = lax.scan(step, s0.astype(jnp.float32), (to_chunks(q), to_chunks(k), to_chunks(v), to_chunks(g)))
    o = o.transpose(1, 0, 3, 2, 4).reshape(b, t, h, v.shape[-1])
    return s_fin, o


def diff_attn_prompt(q, k, v):
    b, t = q.shape[:2]
    qb = math.gcd(t, Q_BLOCK)
    nb = t // qb
    scale = DIFF_HD ** -0.5
    kf = k.astype(jnp.float32)
    vf = v.astype(jnp.float32)
    q_blocks = q.reshape(b, nb, qb, DIFF_HEADS, 2, DIFF_HD).transpose(1, 0, 2, 3, 4, 5)
    key_pos = jnp.arange(t)

    def block(args):
        qblk, i = args
        s = jnp.einsum('bqhjd,bkhjd->bhjqk', qblk.astype(jnp.float32), kf) * scale
        q_pos = i * qb + jnp.arange(qb)
        s = jnp.where(key_pos[None, :] <= q_pos[:, None], s, -jnp.inf)
        pr = jax.nn.softmax(s, axis=-1)
        return jnp.einsum('bhjqk,bkhd->bqhjd', pr, vf)

    o = lax.map(block, (q_blocks, jnp.arange(nb)))
    return o.transpose(1, 0, 2, 3, 4, 5).reshape(b, t, DIFF_HEADS, 2, DIFF_VD)


def diff_attn_paged(q, k, v, cache_k, cache_v, page_table, layer):
    db, t = q.shape[:2]
    scale = DIFF_HD ** -0.5
    qf = q.astype(jnp.float32)
    s = jnp.einsum('bqhjd,bkhjd->bhjqk', qf, k.astype(jnp.float32)) * scale
    s = jnp.where(jnp.tril(jnp.ones((t, t), dtype=bool)), s, -jnp.inf)
    m = jnp.max(s, axis=-1)
    pr = jnp.exp(s - m[..., None])
    l = jnp.sum(pr, axis=-1)
    acc = jnp.einsum('bhjqk,bkhd->bhjqd', pr, v.astype(jnp.float32))
    n_pages = page_table.shape[1]
    pps = math.gcd(n_pages, PAGES_PER_STEP)
    pt = page_table.reshape(db, n_pages // pps, pps).transpose(1, 0, 2)

    def step(carry, pages):
        m, l, acc = carry
        kp = cache_k[layer, pages].reshape(db, pps * PAGE_SIZE, DIFF_HEADS, 2, DIFF_HD).astype(jnp.float32)
        vp = cache_v[layer, pages].reshape(db, pps * PAGE_SIZE, DIFF_HEADS, DIFF_VD).astype(jnp.float32)
        s = jnp.einsum('bqhjd,bkhjd->bhjqk', qf, kp) * scale
        m_new = jnp.maximum(m, jnp.max(s, axis=-1))
        corr = jnp.exp(m - m_new)
        pr = jnp.exp(s - m_new[..., None])
        l = l * corr + jnp.sum(pr, axis=-1)
        acc = acc * corr[..., None] + jnp.einsum('bhjqk,bkhd->bhjqd', pr, vp)
        return (m_new, l, acc), None

    (m, l, acc), _ = lax.scan(step, (m, l, acc), pt)
    o = acc / l[..., None]
    return o.transpose(0, 3, 1, 2, 4)


def mixer_merge(u, o_gla, gr, o_pair, p, lambda_init):
    b, t, _ = u.shape
    lam = (jnp.exp(jnp.sum(p['lambda_q1'].astype(jnp.float32) * p['lambda_k1'].astype(jnp.float32)))
           - jnp.exp(jnp.sum(p['lambda_q2'].astype(jnp.float32) * p['lambda_k2'].astype(jnp.float32)))
           + lambda_init)
    od = o_pair[:, :, :, 0, :] - lam * o_pair[:, :, :, 1, :]
    od = (rms_norm(od, p['g_diff_norm'].astype(jnp.float32)) * (1.0 - lambda_init)).astype(u.dtype)
    od = od.reshape(b, t, DIFF_HEADS * DIFF_VD)
    og = rms_norm(o_gla, p['g_gla_norm'].astype(jnp.float32)).astype(u.dtype).reshape(b, t, GLA_HEADS * GLA_DV)
    og = og * jax.nn.silu(gr)
    gates = jax.nn.sigmoid(jnp.dot(u, p['w_gate']) + p['b_gate'])
    gate_a, gate_b = gates[..., :D_MODEL], gates[..., D_MODEL:]
    merged = gate_a * jnp.dot(og, p['w_br_a']) + gate_b * jnp.dot(od, p['w_br_b'])
    return jnp.dot(merged, p['w_o'])


def mem_kv(mem, p):
    b = mem.shape[0]
    m = rms_norm(mem, p['g_mem'])
    mk = jnp.dot(m, p['w_xk']).reshape(b, N_MEM, X_HEADS, X_HD)
    mv = jnp.dot(m, p['w_xv']).reshape(b, N_MEM, X_HEADS, X_HD)
    return mk, mv


def cross_attend(h, mem_k, mem_v, p):
    b, t, _ = h.shape
    q = jnp.dot(h, p['w_xq']).reshape(b, t, X_HEADS, X_HD)
    s = jnp.einsum('bqhd,bmhd->bhqm', q, mem_k).astype(jnp.float32) * (X_HD ** -0.5)
    pr = jax.nn.softmax(s, axis=-1)
    o = jnp.einsum('bhqm,bmhd->bqhd', pr, mem_v.astype(jnp.float32)).astype(h.dtype)
    return jnp.dot(o.reshape(b, t, X_HEADS * X_HD), p['w_xo'])


def moe(x, w_router, b_router, w_gu, b_gu, w_dn, b_dn):
    m_tok, d = x.shape
    logits = jnp.dot(x, w_router).astype(jnp.float32) + b_router.astype(jnp.float32)
    top_v, top_i = lax.top_k(logits, TOP_K)
    gate = jax.nn.softmax(top_v, axis=-1)
    n_asg = m_tok * TOP_K
    e_flat = top_i.reshape(-1).astype(jnp.int32)
    tok_flat = jnp.arange(n_asg, dtype=jnp.int32) // TOP_K
    order = jnp.argsort(e_flat * n_asg + jnp.arange(n_asg, dtype=jnp.int32))
    e_s = e_flat[order]
    tok_s = tok_flat[order]
    g_s = gate.reshape(-1)[order]
    counts = jnp.zeros((N_EXPERTS,), jnp.int32).at[e_flat].add(1)
    padded = (counts + MOE_BLOCK - 1) // MOE_BLOCK * MOE_BLOCK
    pad_end = jnp.cumsum(padded)
    pad_start = pad_end - padded
    raw_start = jnp.cumsum(counts) - counts
    slot = pad_start[e_s] + jnp.arange(n_asg, dtype=jnp.int32) - raw_start[e_s]
    n_blk = -(-n_asg // MOE_BLOCK) + N_EXPERTS
    slot_tok = jnp.full((n_blk * MOE_BLOCK,), m_tok, jnp.int32).at[slot].set(tok_s)
    slot_g = jnp.zeros((n_blk * MOE_BLOCK,), jnp.float32).at[slot].set(g_s)
    blk_e = jnp.minimum(jnp.searchsorted(pad_end, jnp.arange(n_blk, dtype=jnp.int32) * MOE_BLOCK, side='right'),
                        N_EXPERTS - 1)
    x_pad = jnp.concatenate([x, jnp.zeros((1, d), x.dtype)], axis=0)
    xb = x_pad[slot_tok].reshape(n_blk, MOE_BLOCK, d)

    def expert_block(args):
        xblk, e = args
        h = jnp.dot(xblk, w_gu[e]) + b_gu[e]
        glu, lin = h[:, 0::2], h[:, 1::2]
        glu = jnp.minimum(glu, SWIGLU_LIMIT)
        lin = jnp.clip(lin, -SWIGLU_LIMIT, SWIGLU_LIMIT)
        act = (lin + 1.0) * glu * jax.nn.sigmoid(SWIGLU_ALPHA * glu)
        return jnp.dot(act, w_dn[e]) + b_dn[e]

    yb = lax.map(expert_block, (xb, blk_e))
    rows = (yb.reshape(-1, d) * slot_g[:, None].astype(yb.dtype)).astype(x.dtype)
    return jnp.zeros((m_tok + 1, d), x.dtype).at[slot_tok].add(rows)[:m_tok]


def trunk_layer(x, pos, gla_state0, mem_k, mem_v, diff_attend, p, lambda_init):
    u = rms_norm(x, p['g_norm1'])
    (gq, gk, gv, gg, gr), (dq, dk, dv) = mixer_inputs(u, pos, p)
    gla_state, o_gla = gla_scan(gq, gk, gv, gg, gla_state0)
    o_pair = diff_attend(dq, dk, dv)
    x = x + mixer_merge(u, o_gla, gr, o_pair, p, lambda_init)
    x = x + cross_attend(rms_norm(x, p['g_norm2']), mem_k, mem_v, p)
    h = rms_norm(x, p['g_norm3'])
    b, t, d = h.shape
    x = x + moe(h.reshape(b * t, d), p['w_router'], p['b_router'], p['w_gate_up'], p['b_gate_up'],
                p['w_down'], p['b_down']).reshape(b, t, d)
    return x, dk, dv, gla_state


def setup_inputs(seed: int = 0) -> dict:
    key = jax.random.key(seed)
    ks = iter(jax.random.split(key, 48))

    def nrm(shape, scale):
        return jax.random.normal(next(ks), shape, jnp.float32) * scale

    n_pages = PAST_LEN // PAGE_SIZE
    n_used = DEC_BATCH * n_pages
    n_pool = n_used + max(1, n_used // 4)
    d = D_MODEL
    gla_qk_w = GLA_HEADS * GLA_DK
    gla_v_w = GLA_HEADS * GLA_DV
    diff_v_w = DIFF_HEADS * DIFF_VD
    x_w = X_HEADS * X_HD
    inputs = {}
    inputs['x_prompt'] = nrm((BATCH, SEQ, d), 1.0)
    inputs['x_sample'] = nrm((DEC_BATCH, DEC_SEQ, d), 1.0)
    inputs['cache_k_diff'] = nrm((DEPTH, n_pool, PAGE_SIZE, DIFF_HEADS, 2, DIFF_HD), 1.0)
    inputs['cache_v_diff'] = nrm((DEPTH, n_pool, PAGE_SIZE, DIFF_HEADS, DIFF_VD), 1.0)
    inputs['state_gla'] = nrm((DEPTH, DEC_BATCH, GLA_HEADS, GLA_DK, GLA_DV), 1.0)
    inputs['cache_mem_k'] = nrm((DEPTH, DEC_BATCH, N_MEM, X_HEADS, X_HD), 1.0)
    inputs['cache_mem_v'] = nrm((DEPTH, DEC_BATCH, N_MEM, X_HEADS, X_HD), 1.0)
    inputs['page_table'] = jax.random.permutation(next(ks), n_pool)[:n_used].reshape(DEC_BATCH, n_pages).astype(jnp.int32)
    inputs['mem_prompt'] = nrm((BATCH, N_MEM, d), 1.0)
    inputs['g_norm1'] = 1.0 + nrm((DEPTH, d), 0.02)
    inputs['w_in'] = nrm((DEPTH, d, IN_WIDTH), d ** -0.5)
    inputs['w_gla_a2'] = nrm((DEPTH, GLA_GATE_RANK, gla_qk_w), GLA_GATE_RANK ** -0.5)
    inputs['b_gla_a'] = nrm((DEPTH, gla_qk_w), 0.1)
    inputs['g_gla_norm'] = 1.0 + nrm((DEPTH, GLA_DV), 0.02)
    inputs['lambda_q1'] = nrm((DEPTH, DIFF_HD), 0.1)
    inputs['lambda_k1'] = nrm((DEPTH, DIFF_HD), 0.1)
    inputs['lambda_q2'] = nrm((DEPTH, DIFF_HD), 0.1)
    inputs['lambda_k2'] = nrm((DEPTH, DIFF_HD), 0.1)
    inputs['g_diff_norm'] = 1.0 + nrm((DEPTH, DIFF_VD), 0.02)
    inputs['w_br_a'] = nrm((DEPTH, gla_v_w, d), gla_v_w ** -0.5)
    inputs['w_br_b'] = nrm((DEPTH, diff_v_w, d), diff_v_w ** -0.5)
    inputs['w_gate'] = nrm((DEPTH, d, 2 * d), d ** -0.5)
    inputs['b_gate'] = nrm((DEPTH, 2 * d), 0.1)
    inputs['w_o'] = nrm((DEPTH, d, d), d ** -0.5)
    inputs['g_norm2'] = 1.0 + nrm((DEPTH, d), 0.02)
    inputs['g_mem'] = 1.0 + nrm((DEPTH, d), 0.02)
    inputs['w_xq'] = nrm((DEPTH, d, x_w), d ** -0.5)
    inputs['w_xk'] = nrm((DEPTH, d, x_w), d ** -0.5)
    inputs['w_xv'] = nrm((DEPTH, d, x_w), d ** -0.5)
    inputs['w_xo'] = nrm((DEPTH, x_w, d), x_w ** -0.5)
    inputs['g_norm3'] = 1.0 + nrm((DEPTH, d), 0.02)
    inputs['w_router'] = nrm((DEPTH, d, N_EXPERTS), d ** -0.5)
    inputs['b_router'] = nrm((DEPTH, N_EXPERTS), 0.01)
    inputs['w_gate_up'] = nrm((DEPTH, N_EXPERTS, d, 2 * D_FF), d ** -0.5)
    inputs['b_gate_up'] = nrm((DEPTH, N_EXPERTS, 2 * D_FF), 0.02)
    inputs['w_down'] = nrm((DEPTH, N_EXPERTS, D_FF, d), D_FF ** -0.5)
    inputs['b_down'] = nrm((DEPTH, N_EXPERTS, d), 0.02)
    inputs['g_final'] = 1.0 + nrm((d,), 0.02)
    return inputs


def reference(x_prompt, x_sample, cache_k_diff, cache_v_diff, state_gla, cache_mem_k, cache_mem_v,
              page_table, mem_prompt, g_norm1, w_in, w_gla_a2, b_gla_a, g_gla_norm, lambda_q1, lambda_k1,
              lambda_q2, lambda_k2, g_diff_norm, w_br_a, w_br_b, w_gate, b_gate, w_o, g_norm2, g_mem,
              w_xq, w_xk, w_xv, w_xo, g_norm3, w_router, b_router, w_gate_up, b_gate_up, w_down, b_down,
              g_final):
    b_p, t_p, _ = x_prompt.shape
    t_s = x_sample.shape[1]
    past = page_table.shape[1] * PAGE_SIZE
    pos_p = jnp.arange(t_p, dtype=jnp.float32)
    pos_s = past + jnp.arange(t_s, dtype=jnp.float32)
    xp, xs = x_prompt, x_sample
    kp_l, vp_l, sp_l, mk_l, mv_l, ks_l, vs_l, ss_l = [], [], [], [], [], [], [], []
    for l in range(DEPTH):
        p = dict(g_norm1=g_norm1[l], w_in=w_in[l], w_gla_a2=w_gla_a2[l], b_gla_a=b_gla_a[l],
                 g_gla_norm=g_gla_norm[l], lambda_q1=lambda_q1[l], lambda_k1=lambda_k1[l],
                 lambda_q2=lambda_q2[l], lambda_k2=lambda_k2[l], g_diff_norm=g_diff_norm[l],
                 w_br_a=w_br_a[l], w_br_b=w_br_b[l], w_gate=w_gate[l], b_gate=b_gate[l], w_o=w_o[l],
                 g_norm2=g_norm2[l], g_mem=g_mem[l], w_xq=w_xq[l], w_xk=w_xk[l], w_xv=w_xv[l],
                 w_xo=w_xo[l], g_norm3=g_norm3[l], w_router=w_router[l], b_router=b_router[l],
                 w_gate_up=w_gate_up[l], b_gate_up=b_gate_up[l], w_down=w_down[l], b_down=b_down[l])
        lambda_init = 0.8 - 0.6 * math.exp(-0.3 * l)
        mk, mv = mem_kv(mem_prompt, p)
        s0 = jnp.zeros((b_p, GLA_HEADS, GLA_DK, GLA_DV), jnp.float32)
        xp, kp, vp, sp = trunk_layer(xp, pos_p, s0, mk, mv, diff_attn_prompt, p, lambda_init)
        attend = lambda q, k, v, layer=l: diff_attn_paged(q, k, v, cache_k_diff, cache_v_diff, page_table, layer)
        xs, ks_, vs_, ss = trunk_layer(xs, pos_s, state_gla[l], cache_mem_k[l], cache_mem_v[l], attend, p,
                                       lambda_init)
        kp_l.append(kp)
        vp_l.append(vp)
        sp_l.append(sp.astype(x_prompt.dtype))
        mk_l.append(mk)
        mv_l.append(mv)
        ks_l.append(ks_)
        vs_l.append(vs_)
        ss_l.append(ss.astype(state_gla.dtype))
    y_prompt = rms_norm(xp, g_final)
    y_sample = rms_norm(xs, g_final)
    return (y_prompt, y_sample, jnp.stack(kp_l), jnp.stack(vp_l), jnp.stack(sp_l), jnp.stack(mk_l),
            jnp.stack(mv_l), jnp.stack(ks_l), jnp.stack(vs_l), jnp.stack(ss_l))
```

```python
import functools
import math

import jax
import jax.numpy as jnp
from jax import lax
from jax.experimental import pallas as pl
from jax.experimental.pallas import tpu as pltpu

F32 = jnp.float32
BF16 = jnp.bfloat16

D_MODEL = 1024
GLA_HEADS = 4
GLA_DK = 64
GLA_DV = 128
GLA_GATE_RANK = 16
GLA_GATE_NORM = 16.0
DIFF_HEADS = 4
DIFF_HD = 64
DIFF_VD = 128
X_HEADS = 4
X_HD = 128
N_EXPERTS = 32
TOP_K = 4
D_FF = 1024
SWIGLU_LIMIT = 7.0
SWIGLU_ALPHA = 1.702
PAGE_SIZE = 128
ROPE_THETA = 10000.0
EPS = 1e-6
LAMBDA_INIT = 0.8 - 0.6 * math.exp(-0.3 * 0)

LANES = 128
GLA_CHUNK = 64
GLA_SUB = 16
GLA_EXP_CLAMP = 80.0
VMEM_LIMIT = 56 * 1024 * 1024


def _dot(a, b):
    return jnp.dot(a, b, preferred_element_type=F32)


def _dot_nt(a, b):
    return lax.dot_general(a, b, (((1,), (1,)), ((), ())), preferred_element_type=F32)


def _dot_tn(a, b):
    return lax.dot_general(a, b, (((0,), (0,)), ((), ())), preferred_element_type=F32)


def _rms(x, g):
    return x * lax.rsqrt(jnp.mean(x * x, axis=-1, keepdims=True) + EPS) * g


def _sigmoid(x):
    return 1.0 / (1.0 + jnp.exp(-x))


def _params(*sem):
    return pltpu.CompilerParams(dimension_semantics=sem, vmem_limit_bytes=VMEM_LIMIT)


def _const_spec(shape):
    nd = len(shape)
    return pl.BlockSpec(shape, lambda *_: (0,) * nd)


def _proj_kernel(x_ref, g1_ref, w_ref, wga_ref, wa2_ref, ba_ref, cos_ref, sin_ref,
                 gq_ref, gk_ref, gg_ref, gv_ref, gr_ref, dq_ref, dkf_ref, dkb_ref, dvf_ref, dvb_ref):
    x = x_ref[...]
    u = _rms(x, g1_ref[...]).astype(BF16)

    def seg(lo, hi):
        return _dot(u, w_ref[:, lo:hi])

    gq_ref[...] = seg(0, 256) * (GLA_DK ** -0.5)
    gk_ref[...] = seg(256, 512)
    gv_ref[...] = seg(512, 1024).astype(BF16)
    gr_ref[...] = seg(1024, 1536).astype(BF16)
    ga = _dot(u, wga_ref[...])
    z = _dot(ga.astype(BF16), wa2_ref[...]) + ba_ref[...]
    log_sig = jnp.minimum(z, 0.0) - jnp.log(1.0 + jnp.exp(-jnp.abs(z)))
    gg_ref[...] = log_sig * (1.0 / GLA_GATE_NORM)

    cos = jnp.concatenate([cos_ref[...]] * 4, axis=1)
    sin = jnp.concatenate([sin_ref[...]] * 4, axis=1)
    width = 4 * LANES
    lane = lax.broadcasted_iota(jnp.int32, (1, width), 1)
    first_half = (lane % DIFF_HD) < (DIFF_HD // 2)

    def rope(v):
        partner = jnp.where(first_half, pltpu.roll(v, width - DIFF_HD // 2, 1), pltpu.roll(v, DIFF_HD // 2, 1))
        return v * cos + partner * sin

    dq = rope(seg(1536, 2048))
    dq_ref[...] = (dq * (DIFF_HD ** -0.5)).astype(BF16)
    dk = rope(seg(2048, 2560))
    dkf_ref[...] = dk
    dkb_ref[...] = dk.astype(BF16)
    dv = seg(2560, 3072)
    dvf_ref[...] = dv
    dvb_ref[...] = dv.astype(BF16)


def _proj(x, g1, w_main, w_ga, w_a2, b_a, cos, sin, tm):
    m = x.shape[0]
    n_rope = cos.shape[0] // tm
    tok = lambda w: pl.BlockSpec((tm, w), lambda i: (i, 0))
    outs = [(256, F32), (256, F32), (256, F32), (512, BF16), (512, BF16),
            (512, BF16), (512, F32), (512, BF16), (512, F32), (512, BF16)]
    return pl.pallas_call(
        _proj_kernel,
        grid=(m // tm,),
        in_specs=[tok(D_MODEL), _const_spec((1, D_MODEL)), _const_spec(w_main.shape), _const_spec(w_ga.shape),
                  _const_spec(w_a2.shape), _const_spec((1, 256)),
                  pl.BlockSpec((tm, LANES), lambda i: (i % n_rope, 0)),
                  pl.BlockSpec((tm, LANES), lambda i: (i % n_rope, 0))],
        out_specs=[tok(w) for w, _ in outs],
        out_shape=[jax.ShapeDtypeStruct((m, w), dt) for w, dt in outs],
        compiler_params=_params("parallel"),
    )(x, g1, w_main, w_ga, w_a2, b_a, cos, sin)


def _gla_kernel(q_ref, k_ref, g_ref, v_ref, r_ref, gn_ref, o_ref, s_ref, st_scr, *, chunk, sub):
    c = pl.program_id(1)

    @pl.when(c == 0)
    def _():
        st_scr[...] = jnp.zeros_like(st_scr)

    g = g_ref[...]
    row = lax.broadcasted_iota(jnp.int32, (chunk, chunk), 0)
    col = lax.broadcasted_iota(jnp.int32, (chunk, chunk), 1)
    causal = col <= row
    tril = jnp.where(causal, 1.0, 0.0).astype(BF16)
    g_hi = g.astype(BF16)
    r1 = g - g_hi.astype(F32)
    g_mid = r1.astype(BF16)
    g_lo = (r1 - g_mid.astype(F32)).astype(BF16)
    cum = _dot(tril, g_hi) + _dot(tril, g_mid) + _dot(tril, g_lo)
    q = q_ref[...]
    k = k_ref[...]
    last = cum[chunk - 1:chunk, :]
    q_dec = q * jnp.exp(cum)
    k_dec = k * jnp.exp(last - cum)
    lane = lax.broadcasted_iota(jnp.int32, (1, LANES), 1)
    head_mask = (lane < GLA_DK, lane >= GLA_DK)
    gn = gn_ref[...]

    for pair in range(GLA_HEADS // 2):
        sl = slice(pair * LANES, (pair + 1) * LANES)
        state_t = st_scr[pair]
        cum_p, q_p, k_p = cum[:, sl], q[:, sl], k[:, sl]
        q_dec_p, k_dec_p = q_dec[:, sl], k_dec[:, sl]
        k_anchor = []
        for i in range(chunk // sub):
            anchor = cum_p[i * sub:i * sub + 1, :]
            k_anchor.append((k_p * jnp.exp(jnp.minimum(anchor - cum_p, GLA_EXP_CLAMP))).astype(BF16))
        state_new = jnp.exp(last[:, sl]) * state_t
        state_bf = state_t.astype(BF16)
        for hh in range(2):
            h = 2 * pair + hh
            hm = head_mask[hh]
            v_h = v_ref[:, h * GLA_DV:(h + 1) * GLA_DV]
            rows = []
            for i in range(chunk // sub):
                r0 = i * sub
                q_a = q_p[r0:r0 + sub, :] * jnp.exp(cum_p[r0:r0 + sub, :] - cum_p[r0:r0 + 1, :])
                rows.append(_dot_nt(jnp.where(hm, q_a, 0.0).astype(BF16), k_anchor[i]))
            a = jnp.where(causal, jnp.concatenate(rows, axis=0), 0.0).astype(BF16)
            o = _dot_nt(jnp.where(hm, q_dec_p, 0.0).astype(BF16), state_bf) + _dot(a, v_h)
            state_new = state_new + _dot_tn(v_h, jnp.where(hm, k_dec_p, 0.0).astype(BF16))
            r = r_ref[:, h * GLA_DV:(h + 1) * GLA_DV].astype(F32)
            o_ref[:, h * GLA_DV:(h + 1) * GLA_DV] = (_rms(o, gn) * (r * _sigmoid(r))).astype(BF16)
        st_scr[pair] = state_new

    @pl.when(c == pl.num_programs(1) - 1)
    def _():
        for pair in range(GLA_HEADS // 2):
            s_ref[0, 2 * pair:2 * pair + 2] = st_scr[pair].T.reshape(2, GLA_DK, GLA_DV)


def _gla_prompt(gq, gk, gg, gv, gr, g_gla_norm, batch, seq):
    chunk = math.gcd(seq, GLA_CHUNK)
    sub = math.gcd(chunk, GLA_SUB)
    nc = seq // chunk
    tok = lambda w: pl.BlockSpec((chunk, w), lambda b, c: (b * nc + c, 0))
    return pl.pallas_call(
        functools.partial(_gla_kernel, chunk=chunk, sub=sub),
        grid=(batch, nc),
        in_specs=[tok(256), tok(256), tok(256), tok(512), tok(512), _const_spec((1, GLA_DV))],
        out_specs=[tok(512), pl.BlockSpec((1, GLA_HEADS, GLA_DK, GLA_DV), lambda b, c: (b, 0, 0, 0))],
        out_shape=[jax.ShapeDtypeStruct((batch * seq, 512), BF16),
                   jax.ShapeDtypeStruct((batch, GLA_HEADS, GLA_DK, GLA_DV), F32)],
        scratch_shapes=[pltpu.VMEM((GLA_HEADS // 2, GLA_DV, 2 * GLA_DK), F32)],
        compiler_params=_params("parallel", "arbitrary"),
    )(gq, gk, gg, gv, gr, g_gla_norm)


def _diff_combine(o1, o2, lq1_ref, lk1_ref, lq2_ref, lk2_ref, gdn_ref):
    lam = (jnp.exp(jnp.sum(lq1_ref[...] * lk1_ref[...], axis=-1, keepdims=True))
           - jnp.exp(jnp.sum(lq2_ref[...] * lk2_ref[...], axis=-1, keepdims=True)) + LAMBDA_INIT)
    od = o1 - lam * o2
    return _rms(od, gdn_ref[...]) * (1.0 - LAMBDA_INIT)


def _attn_kernel(qi_tab, ki_tab, q_ref, k_ref, v_ref, lq1_ref, lk1_ref, lq2_ref, lk2_ref, gdn_ref,
                 o_ref, m_scr, l_scr, acc_scr, *, blk):
    step = pl.program_id(2)
    qi = qi_tab[step]
    ki = ki_tab[step]

    @pl.when(ki == 0)
    def _():
        m_scr[...] = jnp.full_like(m_scr, -jnp.inf)
        l_scr[...] = jnp.zeros_like(l_scr)
        acc_scr[...] = jnp.zeros_like(acc_scr)

    lane = lax.broadcasted_iota(jnp.int32, (1, LANES), 1)
    map_mask = (lane < DIFF_HD, lane >= DIFF_HD)

    def update(diagonal):
        q = q_ref[...]
        k = k_ref[...]
        v = v_ref[...]
        if diagonal:
            row = lax.broadcasted_iota(jnp.int32, (blk, blk), 0)
            col = lax.broadcasted_iota(jnp.int32, (blk, blk), 1)
            keep = col <= row
        for j in range(2):
            s = _dot_nt(jnp.where(map_mask[j], q, jnp.zeros_like(q)), k)
            if diagonal:
                s = jnp.where(keep, s, -jnp.inf)
            m_prev = m_scr[j]
            m_new = jnp.maximum(m_prev, jnp.max(s, axis=-1, keepdims=True))
            alpha = jnp.exp(m_prev - m_new)
            p = jnp.exp(s - m_new)
            l_scr[j] = alpha * l_scr[j] + jnp.sum(p, axis=-1, keepdims=True)
            acc_scr[j] = alpha * acc_scr[j] + _dot(p.astype(BF16), v)
            m_scr[j] = m_new

    @pl.when(ki < qi)
    def _():
        update(False)

    @pl.when(ki == qi)
    def _():
        update(True)
        o1 = acc_scr[0] / l_scr[0]
        o2 = acc_scr[1] / l_scr[1]
        o_ref[...] = _diff_combine(o1, o2, lq1_ref, lk1_ref, lq2_ref, lk2_ref, gdn_ref).astype(BF16)


def _diff_attn_prompt(dq, dk, dv, lam_params, g_diff_norm, batch, seq):
    blk = math.gcd(seq, 512)
    nb = seq // blk
    pairs = [(qi, ki) for qi in range(nb) for ki in range(qi + 1)]
    qi_tab = jnp.array([p[0] for p in pairs], jnp.int32)
    ki_tab = jnp.array([p[1] for p in pairs], jnp.int32)
    q_spec = pl.BlockSpec((blk, LANES), lambda b, h, s, qt, kt: (b * nb + qt[s], h))
    kv_spec = pl.BlockSpec((blk, LANES), lambda b, h, s, qt, kt: (b * nb + kt[s], h))
    vec = lambda w: pl.BlockSpec((1, w), lambda b, h, s, qt, kt: (0, 0))
    return pl.pallas_call(
        functools.partial(_attn_kernel, blk=blk),
        grid_spec=pltpu.PrefetchScalarGridSpec(
            num_scalar_prefetch=2,
            grid=(batch, DIFF_HEADS, len(pairs)),
            in_specs=[q_spec, kv_spec, kv_spec, vec(DIFF_HD), vec(DIFF_HD), vec(DIFF_HD), vec(DIFF_HD),
                      vec(DIFF_VD)],
            out_specs=q_spec,
            scratch_shapes=[pltpu.VMEM((2, blk, 1), F32), pltpu.VMEM((2, blk, 1), F32),
                            pltpu.VMEM((2, blk, DIFF_VD), F32)]),
        out_shape=jax.ShapeDtypeStruct((batch * seq, DIFF_HEADS * DIFF_VD), BF16),
        compiler_params=_params("parallel", "parallel", "arbitrary"),
    )(qi_tab, ki_tab, dq, dk, dv, *lam_params, g_diff_norm)


def _merge_kernel(x_ref, og_ref, od_ref, g1_ref, wgate_ref, bgate_ref, wbra_ref, wbrb_ref, wo_ref, g2_ref,
                  wxq_ref, x1_ref, xq_ref):
    x = x_ref[...]
    u = _rms(x, g1_ref[...]).astype(BF16)
    gates = _sigmoid(_dot(u, wgate_ref[...]) + bgate_ref[...])
    merged = (gates[:, :D_MODEL] * _dot(og_ref[...], wbra_ref[...])
              + gates[:, D_MODEL:] * _dot(od_ref[...], wbrb_ref[...]))
    x1 = x + _dot(merged.astype(BF16), wo_ref[...])
    x1_ref[...] = x1
    xq_ref[...] = _dot(_rms(x1, g2_ref[...]).astype(BF16), wxq_ref[...]).astype(BF16)


def _merge(x, og, od, g1, w_gate, b_gate, w_br_a, w_br_b, w_o, g2, w_xq, tm):
    m = x.shape[0]
    tok = lambda w: pl.BlockSpec((tm, w), lambda i: (i, 0))
    consts = [g1, w_gate, b_gate, w_br_a, w_br_b, w_o, g2, w_xq]
    return pl.pallas_call(
        _merge_kernel,
        grid=(m // tm,),
        in_specs=[tok(D_MODEL), tok(512), tok(512)] + [_const_spec(c.shape) for c in consts],
        out_specs=[tok(D_MODEL), tok(512)],
        out_shape=[jax.ShapeDtypeStruct((m, D_MODEL), F32), jax.ShapeDtypeStruct((m, 512), BF16)],
        compiler_params=_params("parallel"),
    )(x, og, od, *consts)


def _post_kernel(*refs, shared_memory):
    if shared_memory:
        (x1_ref, xq_ref, mk_ref, mv_ref, wxo_ref, g3_ref, wrh_ref, wrl_ref, br_ref,
         x2_ref, h3_ref, ti_ref, tg_ref) = refs
        xq = xq_ref[...]
        heads = []
        for h in range(X_HEADS):
            sl = slice(h * X_HD, (h + 1) * X_HD)
            s = _dot_nt(xq[:, sl], mk_ref[:, sl]) * (X_HD ** -0.5)
            p = jnp.exp(s - jnp.max(s, axis=-1, keepdims=True))
            pr = p / jnp.sum(p, axis=-1, keepdims=True)
            heads.append(_dot(pr.astype(BF16), mv_ref[:, sl]).astype(BF16))
        o = jnp.concatenate(heads, axis=1)
    else:
        (x1_ref, o_ref, wxo_ref, g3_ref, wrh_ref, wrl_ref, br_ref, x2_ref, h3_ref, ti_ref, tg_ref) = refs
        o = o_ref[...]
    x2 = x1_ref[...] + _dot(o, wxo_ref[...])
    x2_ref[...] = x2
    h3 = _rms(x2, g3_ref[...])
    h_hi = h3.astype(BF16)
    h3_ref[...] = h_hi
    h_lo = (h3 - h_hi.astype(F32)).astype(BF16)
    logits = (_dot(h_hi, wrh_ref[...]) + _dot(h_lo, wrh_ref[...]) + _dot(h_hi, wrl_ref[...])) + br_ref[...]
    lane = lax.broadcasted_iota(jnp.int32, logits.shape, 1)
    logits = jnp.where(lane < N_EXPERTS, logits, -jnp.inf)
    vals, idxs = [], []
    for _ in range(TOP_K):
        top = jnp.max(logits, axis=-1, keepdims=True)
        idx = jnp.min(jnp.where(logits == top, lane, LANES), axis=-1, keepdims=True)
        vals.append(top)
        idxs.append(idx)
        logits = jnp.where(lane == idx, -jnp.inf, logits)
    exps = [jnp.exp(v - vals[0]) for v in vals]
    denom = exps[0] + exps[1] + exps[2] + exps[3]
    ti = jnp.zeros(lane.shape, jnp.int32)
    tg = jnp.zeros(lane.shape, F32)
    for r in range(TOP_K):
        ti = jnp.where(lane == r, idxs[r], ti)
        tg = jnp.where(lane == r, exps[r] / denom, tg)
    ti_ref[...] = ti
    tg_ref[...] = tg


def _post(x1, attn_in, w_xo, g3, wr_hi, wr_lo, b_r, tm, mem=None, seq=None):
    m = x1.shape[0]
    tok = lambda w: pl.BlockSpec((tm, w), lambda i: (i, 0))
    consts = [w_xo, g3, wr_hi, wr_lo, b_r]
    if mem is not None:
        mk, mv = mem
        n_mem = mk.shape[0] // (m // seq)
        tiles_per_seq = seq // tm
        mem_spec = pl.BlockSpec((n_mem, 512), lambda i: (i // tiles_per_seq, 0))
        in_specs = [tok(D_MODEL), tok(512), mem_spec, mem_spec]
        args = [x1, attn_in, mk, mv]
    else:
        in_specs = [tok(D_MODEL), tok(512)]
        args = [x1, attn_in]
    return pl.pallas_call(
        functools.partial(_post_kernel, shared_memory=mem is not None),
        grid=(m // tm,),
        in_specs=in_specs + [_const_spec(c.shape) for c in consts],
        out_specs=[tok(D_MODEL), tok(D_MODEL), tok(LANES), tok(LANES)],
        out_shape=[jax.ShapeDtypeStruct((m, D_MODEL), F32), jax.ShapeDtypeStruct((m, D_MODEL), BF16),
                   jax.ShapeDtypeStruct((m, LANES), jnp.int32), jax.ShapeDtypeStruct((m, LANES), F32)],
        compiler_params=_params("parallel"),
    )(*args, *consts)


def _memkv_kernel(m_ref, g_ref, wk_ref, wv_ref, kf_ref, vf_ref, kb_ref, vb_ref):
    u = _rms(m_ref[...], g_ref[...]).astype(BF16)
    mk = _dot(u, wk_ref[...])
    mv = _dot(u, wv_ref[...])
    kf_ref[...] = mk
    vf_ref[...] = mv
    kb_ref[...] = mk.astype(BF16)
    vb_ref[...] = mv.astype(BF16)


def _memkv(mem, g_mem, w_xk, w_xv):
    m = mem.shape[0]
    tm = math.gcd(m, 256)
    tok = lambda w: pl.BlockSpec((tm, w), lambda i: (i, 0))
    return pl.pallas_call(
        _memkv_kernel,
        grid=(m // tm,),
        in_specs=[tok(D_MODEL), _const_spec((1, D_MODEL)), _const_spec(w_xk.shape), _const_spec(w_xv.shape)],
        out_specs=[tok(512)] * 4,
        out_shape=[jax.ShapeDtypeStruct((m, 512), F32)] * 2 + [jax.ShapeDtypeStruct((m, 512), BF16)] * 2,
        compiler_params=_params("parallel"),
    )(mem, g_mem, w_xk, w_xv)


def _expert_kernel(blk_e, n_used, x_ref, wgu_ref, bgu_ref, wdn_ref, bdn_ref, y_ref):
    i = pl.program_id(0)

    @pl.when(i < n_used[0])
    def _():
        h = _dot(x_ref[...], wgu_ref[0]) + bgu_ref[0]
        glu = jnp.minimum(h[:, :D_FF], SWIGLU_LIMIT)
        lin = jnp.clip(h[:, D_FF:], -SWIGLU_LIMIT, SWIGLU_LIMIT)
        act = (lin + 1.0) * glu * _sigmoid(SWIGLU_ALPHA * glu)
        y_ref[...] = _dot(act.astype(BF16), wdn_ref[0]) + bdn_ref[0]

    @pl.when(i >= n_used[0])
    def _():
        y_ref[...] = jnp.zeros_like(y_ref)


def _experts(xs, blk_e, n_used, wgu, bgu, wdn, bdn, bm):
    n_blk = xs.shape[0] // bm
    return pl.pallas_call(
        _expert_kernel,
        grid_spec=pltpu.PrefetchScalarGridSpec(
            num_scalar_prefetch=2,
            grid=(n_blk,),
            in_specs=[pl.BlockSpec((bm, D_MODEL), lambda i, be, nu: (i, 0)),
                      pl.BlockSpec((1, D_MODEL, 2 * D_FF), lambda i, be, nu: (be[i], 0, 0)),
                      pl.BlockSpec((1, 1, 2 * D_FF), lambda i, be, nu: (be[i], 0, 0)),
                      pl.BlockSpec((1, D_FF, D_MODEL), lambda i, be, nu: (be[i], 0, 0)),
                      pl.BlockSpec((1, 1, D_MODEL), lambda i, be, nu: (be[i], 0, 0))],
            out_specs=pl.BlockSpec((bm, D_MODEL), lambda i, be, nu: (i, 0))),
        out_shape=jax.ShapeDtypeStruct((n_blk * bm, D_MODEL), F32),
        compiler_params=_params("arbitrary"),
    )(blk_e, n_used, xs, wgu, bgu, wdn, bdn)


def _combine_kernel(x2_ref, y_ref, tg_ref, gf_ref, o_ref):
    tg = tg_ref[...]
    acc = x2_ref[...]
    for r in range(TOP_K):
        acc = acc + tg[:, r:r + 1] * y_ref[:, r * D_MODEL:(r + 1) * D_MODEL]
    o_ref[...] = _rms(acc, gf_ref[...])


def _combine(x2, y_tok, tg, g_final, tm):
    m = x2.shape[0]
    tok = lambda w: pl.BlockSpec((tm, w), lambda i: (i, 0))
    return pl.pallas_call(
        _combine_kernel,
        grid=(m // tm,),
        in_specs=[tok(D_MODEL), tok(TOP_K * D_MODEL), tok(LANES), _const_spec((1, D_MODEL))],
        out_specs=tok(D_MODEL),
        out_shape=jax.ShapeDtypeStruct((m, D_MODEL), F32),
        compiler_params=_params("parallel"),
    )(x2, y_tok, tg, g_final)


def _moe(x2, h3, ti, tg, moe_w, g_final, tm, bm):
    wgu, bgu, wdn, bdn = moe_w
    m = x2.shape[0]
    n_asg = m * TOP_K
    e = ti[:, :TOP_K]
    onehot = (e[:, :, None] == jnp.arange(N_EXPERTS, dtype=jnp.int32)).astype(jnp.int32)
    per_tok = jnp.sum(onehot, axis=1)
    before = jnp.cumsum(per_tok, axis=0) - per_tok
    counts = jnp.sum(per_tok, axis=0)
    padded = (counts + bm - 1) // bm * bm
    pad_end = jnp.cumsum(padded)
    pad_start = pad_end - padded
    slot = jnp.sum(onehot * (before + pad_start)[:, None, :], axis=-1)
    n_blk = -(-n_asg // bm) + N_EXPERTS
    blk_start = jnp.arange(n_blk, dtype=jnp.int32) * bm
    blk_e = jnp.minimum(jnp.sum((blk_start[:, None] >= pad_end[None, :]).astype(jnp.int32), axis=1),
                        N_EXPERTS - 1).astype(jnp.int32)
    n_used = (pad_end[-1:] // bm).astype(jnp.int32)
    slot_tok = jnp.full((n_blk * bm,), m, jnp.int32).at[slot.reshape(-1)].set(
        jnp.arange(n_asg, dtype=jnp.int32) // TOP_K)
    h_pad = jnp.concatenate([h3, jnp.zeros((1, D_MODEL), h3.dtype)], axis=0)
    xs = h_pad[slot_tok]
    ys = _experts(xs, blk_e, n_used, wgu, bgu, wdn, bdn, bm)
    y_tok = ys[slot.reshape(-1)].reshape(m, TOP_K * D_MODEL)
    return _combine(x2, y_tok, tg, g_final, tm)


def _gla_step_kernel(s0_ref, dcol_ref, kcol_ref, qcol_ref, v_ref, r_ref, gn_ref, s_ref, o_ref):
    s_new = jnp.exp(dcol_ref[...]) * s0_ref[...] + kcol_ref[...] * v_ref[...]
    s_ref[...] = s_new
    o = jnp.sum(qcol_ref[...] * s_new, axis=2)
    r = r_ref[...].astype(F32)
    o_ref[...] = (_rms(o, gn_ref[...]) * (r * _sigmoid(r))).astype(BF16)


def _gla_step(s0, gq, gk, gg, gv, gr, g_gla_norm):
    db = s0.shape[0]
    tb = math.gcd(db, 8)
    col = lambda a: jnp.broadcast_to(a.reshape(db, GLA_HEADS, GLA_DK, 1), (db, GLA_HEADS, GLA_DK, GLA_DV))
    big = pl.BlockSpec((tb, GLA_HEADS, GLA_DK, GLA_DV), lambda i: (i, 0, 0, 0))
    return pl.pallas_call(
        _gla_step_kernel,
        grid=(db // tb,),
        in_specs=[big, big, big, big,
                  pl.BlockSpec((tb, GLA_HEADS, 1, GLA_DV), lambda i: (i, 0, 0, 0)),
                  pl.BlockSpec((tb, GLA_HEADS, GLA_DV), lambda i: (i, 0, 0)),
                  pl.BlockSpec((1, 1, GLA_DV), lambda i: (0, 0, 0))],
        out_specs=[big, pl.BlockSpec((tb, GLA_HEADS, GLA_DV), lambda i: (i, 0, 0))],
        out_shape=[jax.ShapeDtypeStruct(s0.shape, F32), jax.ShapeDtypeStruct((db, GLA_HEADS, GLA_DV), BF16)],
        compiler_params=_params("parallel"),
    )(s0, col(gg), col(gk), col(gq), gv.reshape(db, GLA_HEADS, 1, GLA_DV), gr.reshape(db, GLA_HEADS, GLA_DV),
      g_gla_norm.reshape(1, 1, GLA_DV))


def _paged_kernel(pt_ref, q_ref, kn_ref, vn_ref, lq1_ref, lk1_ref, lq2_ref, lk2_ref, gdn_ref, *rest, pages):
    k_refs = rest[:pages]
    v_refs = rest[pages:2 * pages]
    o_ref, qm_scr, m_scr, l_scr, acc_scr = rest[2 * pages:]
    step = pl.program_id(1)
    width = DIFF_HEADS * 2 * DIFF_HD
    n_maps = 2 * DIFF_HEADS

    @pl.when(step == 0)
    def _():
        row = lax.broadcasted_iota(jnp.int32, (n_maps, width), 0)
        lane = lax.broadcasted_iota(jnp.int32, (n_maps, width), 1)
        q_rows = jnp.where(lane // DIFF_HD == row, jnp.broadcast_to(q_ref[0], (n_maps, width)), 0.0)
        qm_scr[...] = q_rows
        m_scr[...] = jnp.sum(q_rows * kn_ref[0], axis=-1, keepdims=True)
        l_scr[...] = jnp.ones_like(l_scr)
        acc_scr[...] = jnp.broadcast_to(vn_ref[0], acc_scr.shape)

    qm = qm_scr[...].astype(BF16)
    s = jnp.concatenate([_dot_nt(qm, k_refs[i][0].astype(BF16)) for i in range(pages)], axis=1)
    m_prev = m_scr[...]
    m_new = jnp.maximum(m_prev, jnp.max(s, axis=-1, keepdims=True))
    alpha = jnp.exp(m_prev - m_new)
    p = jnp.exp(s - m_new)
    l_scr[...] = alpha * l_scr[...] + jnp.sum(p, axis=-1, keepdims=True)
    acc = alpha * acc_scr[...]
    for i in range(pages):
        acc = acc + _dot(p[:, i * PAGE_SIZE:(i + 1) * PAGE_SIZE].astype(BF16), v_refs[i][0].astype(BF16))
    acc_scr[...] = acc
    m_scr[...] = m_new

    @pl.when(step == pl.num_programs(1) - 1)
    def _():
        o = acc_scr[...] / l_scr[...]
        for h in range(DIFF_HEADS):
            sl = slice(h * DIFF_VD, (h + 1) * DIFF_VD)
            od = _diff_combine(o[2 * h:2 * h + 1, sl], o[2 * h + 1:2 * h + 2, sl],
                               lq1_ref, lk1_ref, lq2_ref, lk2_ref, gdn_ref)
            o_ref[0, :, sl] = od.astype(BF16)


def _diff_attn_paged(dq, dk, dv, cache_k, cache_v, page_table, lam_params, g_diff_norm):
    db, n_pages = page_table.shape
    pages = math.gcd(n_pages, 16)
    width = DIFF_HEADS * 2 * DIFF_HD
    row = pl.BlockSpec((1, 1, width), lambda b, s, pt: (b, 0, 0))
    vec = lambda w: pl.BlockSpec((1, w), lambda b, s, pt: (0, 0))

    def page_spec(i):
        return pl.BlockSpec((1, PAGE_SIZE, width), lambda b, s, pt: (pt[b, s * pages + i], 0, 0))

    n_maps = 2 * DIFF_HEADS
    return pl.pallas_call(
        functools.partial(_paged_kernel, pages=pages),
        grid_spec=pltpu.PrefetchScalarGridSpec(
            num_scalar_prefetch=1,
            grid=(db, n_pages // pages),
            in_specs=[row, row, row, vec(DIFF_HD), vec(DIFF_HD), vec(DIFF_HD), vec(DIFF_HD), vec(DIFF_VD)]
            + [page_spec(i) for i in range(pages)] * 2,
            out_specs=row,
            scratch_shapes=[pltpu.VMEM((n_maps, width), F32), pltpu.VMEM((n_maps, 1), F32),
                            pltpu.VMEM((n_maps, 1), F32), pltpu.VMEM((n_maps, width), F32)]),
        out_shape=jax.ShapeDtypeStruct((db, 1, width), BF16),
        compiler_params=_params("parallel", "arbitrary"),
    )(page_table, dq.reshape(db, 1, width), dk.reshape(db, 1, width), dv.reshape(db, 1, width), *lam_params,
      g_diff_norm, *([cache_k] * pages), *([cache_v] * pages))


def _cross_sample_kernel(q_ref, mk_ref, mv_ref, o_ref):
    width = X_HEADS * X_HD
    rows = 8
    row = lax.broadcasted_iota(jnp.int32, (rows, width), 0)
    lane = lax.broadcasted_iota(jnp.int32, (rows, width), 1)
    q = jnp.broadcast_to(q_ref[0].astype(F32), (rows, width))
    q_rows = jnp.where(lane // X_HD == row, q, 0.0).astype(BF16)
    s = _dot_nt(q_rows, mk_ref[0].astype(BF16)) * (X_HD ** -0.5)
    p = jnp.exp(s - jnp.max(s, axis=-1, keepdims=True))
    pr = p / jnp.sum(p, axis=-1, keepdims=True)
    o = _dot(pr.astype(BF16), mv_ref[0].astype(BF16))
    for h in range(X_HEADS):
        sl = slice(h * X_HD, (h + 1) * X_HD)
        o_ref[0, :, sl] = o[h:h + 1, sl].astype(BF16)


def _cross_sample(xq, mem_k, mem_v):
    db, n_mem, width = mem_k.shape
    row = pl.BlockSpec((1, 1, width), lambda b: (b, 0, 0))
    mem = pl.BlockSpec((1, n_mem, width), lambda b: (b, 0, 0))
    return pl.pallas_call(
        _cross_sample_kernel,
        grid=(db,),
        in_specs=[row, mem, mem],
        out_specs=row,
        out_shape=jax.ShapeDtypeStruct((db, 1, width), BF16),
        compiler_params=_params("parallel"),
    )(xq.reshape(db, 1, width), mem_k, mem_v)


def _rope_tables(pos):
    half = DIFF_HD // 2
    inv = jnp.exp(-math.log(ROPE_THETA) * jnp.arange(half, dtype=F32) * (2.0 / DIFF_HD))
    ang = pos[:, None] * inv[None, :]
    cos, sin = jnp.cos(ang), jnp.sin(ang)
    cos_t = jnp.concatenate([cos, cos, cos, cos], axis=1)
    sin_t = jnp.concatenate([-sin, sin, -sin, sin], axis=1)
    return cos_t, sin_t


def _forward(x_prompt, x_sample, cache_k_diff, cache_v_diff, state_gla, cache_mem_k, cache_mem_v, page_table,
             mem_prompt, g_norm1, w_in, w_gla_a2, b_gla_a, g_gla_norm, lambda_q1, lambda_k1, lambda_q2,
             lambda_k2, g_diff_norm, w_br_a, w_br_b, w_gate, b_gate, w_o, g_norm2, g_mem, w_xq, w_xk, w_xv,
             w_xo, g_norm3, w_router, b_router, w_gate_up, b_gate_up, w_down, b_down, g_final):
    b_p, t_p, d = x_prompt.shape
    db, t_s, _ = x_sample.shape
    assert t_s == 1 and d == D_MODEL and w_in.shape[0] == 1
    n_pages = page_table.shape[1]
    n_pool = cache_k_diff.shape[1]
    n_mem = mem_prompt.shape[1]
    row = lambda a: a.reshape(1, -1).astype(F32)

    wi = w_in[0]
    w_main = jnp.concatenate([wi[:, :1536], wi[:, 1552:]], axis=1).astype(BF16)
    w_ga = jnp.pad(wi[:, 1536:1552], ((0, 0), (0, LANES - GLA_GATE_RANK))).astype(BF16)
    w_a2 = jnp.pad(w_gla_a2[0], ((0, LANES - GLA_GATE_RANK), (0, 0))).astype(BF16)
    g1, g2, g3, gm, gf = row(g_norm1[0]), row(g_norm2[0]), row(g_norm3[0]), row(g_mem[0]), row(g_final)
    lam_params = [row(lambda_q1[0]), row(lambda_k1[0]), row(lambda_q2[0]), row(lambda_k2[0])]
    gdn, ggn = row(g_diff_norm[0]), row(g_gla_norm[0])
    merge_w = [g1, w_gate[0].astype(BF16), row(b_gate[0]), w_br_a[0].astype(BF16), w_br_b[0].astype(BF16),
               w_o[0].astype(BF16), g2, w_xq[0].astype(BF16)]
    wr = jnp.pad(w_router[0], ((0, 0), (0, LANES - N_EXPERTS)))
    wr_hi = wr.astype(BF16)
    wr_lo = (wr - wr_hi.astype(F32)).astype(BF16)
    post_w = [w_xo[0].astype(BF16), g3, wr_hi, wr_lo, row(jnp.pad(b_router[0], (0, LANES - N_EXPERTS)))]
    wgu = w_gate_up[0].reshape(N_EXPERTS, d, D_FF, 2).transpose(0, 1, 3, 2).reshape(N_EXPERTS, d, 2 * D_FF)
    bgu = b_gate_up[0].reshape(N_EXPERTS, D_FF, 2).transpose(0, 2, 1).reshape(N_EXPERTS, 1, 2 * D_FF)
    moe_w = [wgu.astype(BF16), bgu.astype(F32), w_down[0].astype(BF16), b_down[0].reshape(N_EXPERTS, 1, d)]

    m_p = b_p * t_p
    tm_p = math.gcd(t_p, 512)
    xp = x_prompt.reshape(m_p, d)
    mkf, mvf, mkb, mvb = _memkv(mem_prompt.reshape(b_p * n_mem, d), gm, w_xk[0].astype(BF16),
                                w_xv[0].astype(BF16))
    cos_p, sin_p = _rope_tables(jnp.arange(t_p, dtype=F32))
    gq, gk, gg, gv, gr, dq, dkf, dkb, dvf, dvb = _proj(xp, g1, w_main, w_ga, w_a2, row(b_gla_a[0]), cos_p, sin_p,
                                                      tm_p)
    og, gla_p = _gla_prompt(gq, gk, gg, gv, gr, ggn, b_p, t_p)
    od = _diff_attn_prompt(dq, dkb, dvb, lam_params, gdn, b_p, t_p)
    x1, xq = _merge(xp, og, od, *merge_w, tm_p)
    x2, h3, ti, tg = _post(x1, xq, *post_w, tm_p, mem=(mkb, mvb), seq=t_p)
    y_p = _moe(x2, h3, ti, tg, moe_w, gf, tm_p, 256)

    tm_s = db
    xs = x_sample.reshape(db, d)
    past = n_pages * PAGE_SIZE
    cos_s, sin_s = _rope_tables(jnp.full((db,), past, dtype=F32))
    sq, sk, sg, sv, sr, sdq, sdkf, sdkb, sdvf, sdvb = _proj(xs, g1, w_main, w_ga, w_a2, row(b_gla_a[0]), cos_s,
                                                            sin_s, tm_s)
    gla_s, og_s = _gla_step(state_gla[0], sq, sk, sg, sv, sr, ggn)
    width = DIFF_HEADS * 2 * DIFF_HD
    od_s = _diff_attn_paged(sdq.astype(F32), sdkf, sdvf, cache_k_diff[0].reshape(n_pool, PAGE_SIZE, width),
                            cache_v_diff[0].reshape(n_pool, PAGE_SIZE, width), page_table, lam_params, gdn)
    x1s, xqs = _merge(xs, og_s.reshape(db, 512), od_s.reshape(db, 512), *merge_w, tm_s)
    o_s = _cross_sample(xqs, cache_mem_k[0].reshape(db, n_mem, X_HEADS * X_HD),
                        cache_mem_v[0].reshape(db, n_mem, X_HEADS * X_HD))
    x2s, h3s, tis, tgs = _post(x1s, o_s.reshape(db, 512), *post_w, tm_s)
    y_s = _moe(x2s, h3s, tis, tgs, moe_w, gf, tm_s, 128)

    return (y_p.reshape(b_p, t_p, d), y_s.reshape(db, 1, d),
            dkf.reshape(1, b_p, t_p, DIFF_HEADS, 2, DIFF_HD), dvf.reshape(1, b_p, t_p, DIFF_HEADS, DIFF_VD),
            gla_p[None], mkf.reshape(1, b_p, n_mem, X_HEADS, X_HD), mvf.reshape(1, b_p, n_mem, X_HEADS, X_HD),
            sdkf.reshape(1, db, 1, DIFF_HEADS, 2, DIFF_HD), sdvf.reshape(1, db, 1, DIFF_HEADS, DIFF_VD),
            gla_s[None])


def kernel(x_prompt, x_sample, cache_k_diff, cache_v_diff, state_gla, cache_mem_k, cache_mem_v, page_table, mem_prompt, g_norm1, w_in, w_gla_a2, b_gla_a, g_gla_norm, lambda_q1, lambda_k1, lambda_q2, lambda_k2, g_diff_norm, w_br_a, w_br_b, w_gate, b_gate, w_o, g_norm2, g_mem, w_xq, w_xk, w_xv, w_xo, g_norm3, w_router, b_router, w_gate_up, b_gate_up, w_down, b_down, g_final):
    return _forward(x_prompt, x_sample, cache_k_diff, cache_v_diff, state_gla, cache_mem_k, cache_mem_v,
                    page_table, mem_prompt, g_norm1, w_in, w_gla_a2, b_gla_a, g_gla_norm, lambda_q1, lambda_k1,
                    lambda_q2, lambda_k2, g_diff_norm, w_br_a, w_br_b, w_gate, b_gate, w_o, g_norm2, g_mem,
                    w_xq, w_xk, w_xv, w_xo, g_norm3, w_router, b_router, w_gate_up, b_gate_up, w_down, b_down,
                    g_final)
```

```python
import functools
import math

import jax
import jax.numpy as jnp
from jax import lax
from jax.experimental import pallas as pl
from jax.experimental.pallas import tpu as pltpu

F32 = jnp.float32
BF16 = jnp.bfloat16

D_MODEL = 1024
GLA_HEADS = 4
GLA_DK = 64
GLA_DV = 128
GLA_GATE_RANK = 16
GLA_GATE_NORM = 16.0
DIFF_HEADS = 4
DIFF_HD = 64
DIFF_VD = 128
X_HEADS = 4
X_HD = 128
N_EXPERTS = 32
TOP_K = 4
D_FF = 1024
SWIGLU_LIMIT = 7.0
SWIGLU_ALPHA = 1.702
PAGE_SIZE = 128
ROPE_THETA = 10000.0
EPS = 1e-6
LAMBDA_INIT = 0.8 - 0.6 * math.exp(-0.3 * 0)

LANES = 128
ROW_TILE = 8
LOG2E = math.log2(math.e)
GLA_CHUNK = 64
GLA_SUB = 16
GLA_EXP_CLAMP = 80.0
VMEM_LIMIT = 56 * 1024 * 1024


def _dot(a, b):
    return jnp.dot(a, b, preferred_element_type=F32)


def _dot_nt(a, b):
    return lax.dot_general(a, b, (((1,), (1,)), ((), ())), preferred_element_type=F32)


def _dot_tn(a, b):
    return lax.dot_general(a, b, (((0,), (0,)), ((), ())), preferred_element_type=F32)


def _rms(x, g):
    return x * lax.rsqrt(jnp.mean(x * x, axis=-1, keepdims=True) + EPS) * g


def _sigmoid(x):
    return 1.0 / (1.0 + jnp.exp(-x))


def _params(*sem):
    return pltpu.CompilerParams(dimension_semantics=sem, vmem_limit_bytes=VMEM_LIMIT)


def _const_spec(shape):
    nd = len(shape)
    return pl.BlockSpec(shape, lambda *_: (0,) * nd)


def _proj_kernel(x_ref, g1_ref, w_ref, wkt_ref, wga_ref, wa2_ref, ba_ref, cos_ref, sin_ref, cost_ref, sint_ref,
                 gq_ref, gk_ref, gg_ref, gv_ref, gr_ref, dq_ref, dvf_ref, dvb_ref, dkf_ref, dkb_ref):
    x = x_ref[...]
    u = _rms(x, g1_ref[...]).astype(BF16)

    def seg(lo, hi):
        return _dot(u, w_ref[:, lo:hi])

    gq_ref[...] = seg(0, 256) * (GLA_DK ** -0.5)
    gk_ref[...] = seg(256, 512)
    gv_ref[...] = seg(512, 1024).astype(BF16)
    gr_ref[...] = seg(1024, 1536).astype(BF16)
    ga = _dot(u, wga_ref[...])
    z = _dot(ga.astype(BF16), wa2_ref[...]) + ba_ref[...]
    log_sig = jnp.minimum(z, 0.0) - jnp.log(1.0 + jnp.exp(-jnp.abs(z)))
    gg_ref[...] = log_sig * (1.0 / GLA_GATE_NORM)

    cos = jnp.concatenate([cos_ref[...]] * 4, axis=1)
    sin = jnp.concatenate([sin_ref[...]] * 4, axis=1)
    width = 4 * LANES
    lane = lax.broadcasted_iota(jnp.int32, (1, width), 1)
    first_half = (lane % DIFF_HD) < (DIFF_HD // 2)

    def rope(v):
        partner = jnp.where(first_half, pltpu.roll(v, width - DIFF_HD // 2, 1), pltpu.roll(v, DIFF_HD // 2, 1))
        return v * cos + partner * sin

    dq = rope(seg(1536, 2048))
    dq_ref[...] = (dq * (DIFF_HD ** -0.5 * LOG2E)).astype(BF16)
    dv = seg(2048, 2560)
    dvf_ref[...] = dv
    dvb_ref[...] = dv.astype(BF16)
    k_t = _dot_nt(wkt_ref[...], u)
    cos_t = cost_ref[...]
    sin_t = sint_ref[...]
    half = DIFF_HD // 2
    slabs = []
    for grp in range(2 * DIFF_HEADS):
        x1 = k_t[grp * DIFF_HD:grp * DIFF_HD + half, :]
        x2 = k_t[grp * DIFF_HD + half:(grp + 1) * DIFF_HD, :]
        slabs += [x1 * cos_t - x2 * sin_t, x2 * cos_t + x1 * sin_t]
    k_rot = jnp.concatenate(slabs, axis=0)
    dkf_ref[0] = k_rot
    dkb_ref[0] = k_rot.astype(BF16)


def _proj(x, g1, w_main, w_kt, w_ga, w_a2, b_a, rope_tables, tm, seq):
    m = x.shape[0]
    cos, sin, cos_t, sin_t = rope_tables
    n_seq = seq // tm
    width = DIFF_HEADS * 2 * DIFF_HD
    tok = lambda w: pl.BlockSpec((tm, w), lambda i: (i, 0))
    rope = pl.BlockSpec((tm, LANES), lambda i: (i % n_seq, 0))
    rope_t = pl.BlockSpec((DIFF_HD // 2, tm), lambda i: (0, i % n_seq))
    key_t = pl.BlockSpec((1, width, tm), lambda i: (i // n_seq, 0, i % n_seq))
    outs = [(256, F32), (256, F32), (256, F32), (512, BF16), (512, BF16), (512, BF16), (512, F32), (512, BF16)]
    return pl.pallas_call(
        _proj_kernel,
        grid=(m // tm,),
        in_specs=[tok(D_MODEL), _const_spec((1, D_MODEL)), _const_spec(w_main.shape), _const_spec(w_kt.shape),
                  _const_spec(w_ga.shape), _const_spec(w_a2.shape), _const_spec((1, 256)),
                  rope, rope, rope_t, rope_t],
        out_specs=[tok(w) for w, _ in outs] + [key_t, key_t],
        out_shape=[jax.ShapeDtypeStruct((m, w), dt) for w, dt in outs]
        + [jax.ShapeDtypeStruct((m // seq, width, seq), F32), jax.ShapeDtypeStruct((m // seq, width, seq), BF16)],
        compiler_params=_params("parallel"),
    )(x, g1, w_main, w_kt, w_ga, w_a2, b_a, cos, sin, cos_t, sin_t)


def _gla_kernel(q_ref, k_ref, g_ref, v_ref, r_ref, gn_ref, o_ref, s_ref, st_scr, *, chunk, sub):
    c = pl.program_id(1)

    @pl.when(c == 0)
    def _():
        st_scr[...] = jnp.zeros_like(st_scr)

    g = g_ref[...]
    row = lax.broadcasted_iota(jnp.int32, (chunk, chunk), 0)
    col = lax.broadcasted_iota(jnp.int32, (chunk, chunk), 1)
    causal = col <= row
    tril = jnp.where(causal, 1.0, 0.0).astype(BF16)
    g_hi = g.astype(BF16)
    r1 = g - g_hi.astype(F32)
    g_mid = r1.astype(BF16)
    g_lo = (r1 - g_mid.astype(F32)).astype(BF16)
    cum = _dot(tril, g_hi) + _dot(tril, g_mid) + _dot(tril, g_lo)
    q = q_ref[...]
    k = k_ref[...]
    last = cum[chunk - 1:chunk, :]
    q_dec = q * jnp.exp(cum)
    k_dec = k * jnp.exp(last - cum)
    lane = lax.broadcasted_iota(jnp.int32, (1, LANES), 1)
    head_mask = (lane < GLA_DK, lane >= GLA_DK)
    gn = gn_ref[...]

    for pair in range(GLA_HEADS // 2):
        sl = slice(pair * LANES, (pair + 1) * LANES)
        state_t = st_scr[pair]
        cum_p, q_p, k_p = cum[:, sl], q[:, sl], k[:, sl]
        q_dec_p, k_dec_p = q_dec[:, sl], k_dec[:, sl]
        k_anchor = []
        for i in range(chunk // sub):
            anchor = cum_p[i * sub:i * sub + 1, :]
            k_anchor.append((k_p * jnp.exp(jnp.minimum(anchor - cum_p, GLA_EXP_CLAMP))).astype(BF16))
        state_new = jnp.exp(last[:, sl]) * state_t
        state_bf = state_t.astype(BF16)
        for hh in range(2):
            h = 2 * pair + hh
            hm = head_mask[hh]
            v_h = v_ref[:, h * GLA_DV:(h + 1) * GLA_DV]
            rows = []
            for i in range(chunk // sub):
                r0 = i * sub
                q_a = q_p[r0:r0 + sub, :] * jnp.exp(cum_p[r0:r0 + sub, :] - cum_p[r0:r0 + 1, :])
                rows.append(_dot_nt(jnp.where(hm, q_a, 0.0).astype(BF16), k_anchor[i]))
            a = jnp.where(causal, jnp.concatenate(rows, axis=0), 0.0).astype(BF16)
            o = _dot_nt(jnp.where(hm, q_dec_p, 0.0).astype(BF16), state_bf) + _dot(a, v_h)
            state_new = state_new + _dot_tn(v_h, jnp.where(hm, k_dec_p, 0.0).astype(BF16))
            r = r_ref[:, h * GLA_DV:(h + 1) * GLA_DV].astype(F32)
            o_ref[:, h * GLA_DV:(h + 1) * GLA_DV] = (_rms(o, gn) * (r * _sigmoid(r))).astype(BF16)
        st_scr[pair] = state_new

    @pl.when(c == pl.num_programs(1) - 1)
    def _():
        for pair in range(GLA_HEADS // 2):
            s_ref[0, 2 * pair:2 * pair + 2] = st_scr[pair].T.reshape(2, GLA_DK, GLA_DV)


def _gla_prompt(gq, gk, gg, gv, gr, g_gla_norm, batch, seq):
    chunk = math.gcd(seq, GLA_CHUNK)
    sub = math.gcd(chunk, GLA_SUB)
    nc = seq // chunk
    tok = lambda w: pl.BlockSpec((chunk, w), lambda b, c: (b * nc + c, 0))
    return pl.pallas_call(
        functools.partial(_gla_kernel, chunk=chunk, sub=sub),
        grid=(batch, nc),
        in_specs=[tok(256), tok(256), tok(256), tok(512), tok(512), _const_spec((1, GLA_DV))],
        out_specs=[tok(512), pl.BlockSpec((1, GLA_HEADS, GLA_DK, GLA_DV), lambda b, c: (b, 0, 0, 0))],
        out_shape=[jax.ShapeDtypeStruct((batch * seq, 512), BF16),
                   jax.ShapeDtypeStruct((batch, GLA_HEADS, GLA_DK, GLA_DV), F32)],
        scratch_shapes=[pltpu.VMEM((GLA_HEADS // 2, GLA_DV, 2 * GLA_DK), F32)],
        compiler_params=_params("parallel", "arbitrary"),
    )(gq, gk, gg, gv, gr, g_gla_norm)


def _diff_combine(o1, o2, lq1_ref, lk1_ref, lq2_ref, lk2_ref, gdn_ref):
    lam = (jnp.exp(jnp.sum(lq1_ref[...] * lk1_ref[...], axis=-1, keepdims=True))
           - jnp.exp(jnp.sum(lq2_ref[...] * lk2_ref[...], axis=-1, keepdims=True)) + LAMBDA_INIT)
    od = o1 - lam * o2
    return _rms(od, gdn_ref[...]) * (1.0 - LAMBDA_INIT)


def _attn_kernel(qi_tab, ki_tab, q_ref, k_ref, v_ref, lq1_ref, lk1_ref, lq2_ref, lk2_ref, gdn_ref,
                 o_ref, m_scr, l_scr, acc_scr, *, blk):
    step = pl.program_id(2)
    qi = qi_tab[step]
    ki = ki_tab[step]

    @pl.when(ki == 0)
    def _():
        m_scr[...] = jnp.full_like(m_scr, -jnp.inf)
        l_scr[...] = jnp.zeros_like(l_scr)
        acc_scr[...] = jnp.zeros_like(acc_scr)

    lane = lax.broadcasted_iota(jnp.int32, (1, LANES), 1)
    map_mask = (lane < DIFF_HD, lane >= DIFF_HD)

    def update(diagonal):
        q = q_ref[...]
        k_t = k_ref[0]
        v = v_ref[...]
        if diagonal:
            row = lax.broadcasted_iota(jnp.int32, (blk, blk), 0)
            col = lax.broadcasted_iota(jnp.int32, (blk, blk), 1)
            keep = col <= row
        scores = [_dot(jnp.where(map_mask[j], q, jnp.zeros_like(q)), k_t) for j in range(2)]
        for j in range(2):
            s = scores[j]
            if diagonal:
                s = jnp.where(keep, s, -jnp.inf)
            m_prev = m_scr[j]
            m_new = jnp.maximum(m_prev, jnp.max(s, axis=-1, keepdims=True))
            alpha = jnp.exp2(m_prev - m_new)
            p = jnp.exp2(s - jnp.concatenate([m_new] * (blk // LANES), axis=1))
            l_scr[j] = alpha * l_scr[j] + jnp.sum(p, axis=-1, keepdims=True)
            acc_scr[j] = alpha * acc_scr[j] + _dot(p.astype(BF16), v)
            m_scr[j] = m_new

    @pl.when(ki < qi)
    def _():
        update(False)

    @pl.when(ki == qi)
    def _():
        update(True)
        o1 = acc_scr[0] / l_scr[0]
        o2 = acc_scr[1] / l_scr[1]
        o_ref[...] = _diff_combine(o1, o2, lq1_ref, lk1_ref, lq2_ref, lk2_ref, gdn_ref).astype(BF16)


def _diff_attn_prompt(dq, dk, dv, lam_params, g_diff_norm, batch, seq):
    blk = math.gcd(seq, 512)
    nb = seq // blk
    pairs = [(qi, ki) for qi in range(nb) for ki in range(qi + 1)]
    qi_tab = jnp.array([p[0] for p in pairs], jnp.int32)
    ki_tab = jnp.array([p[1] for p in pairs], jnp.int32)
    q_spec = pl.BlockSpec((blk, LANES), lambda b, h, s, qt, kt: (b * nb + qt[s], h))
    kv_spec = pl.BlockSpec((blk, LANES), lambda b, h, s, qt, kt: (b * nb + kt[s], h))
    kt_spec = pl.BlockSpec((1, LANES, blk), lambda b, h, s, qt, kt: (b, h, kt[s]))
    vec = lambda w: pl.BlockSpec((1, w), lambda b, h, s, qt, kt: (0, 0))
    return pl.pallas_call(
        functools.partial(_attn_kernel, blk=blk),
        grid_spec=pltpu.PrefetchScalarGridSpec(
            num_scalar_prefetch=2,
            grid=(batch, DIFF_HEADS, len(pairs)),
            in_specs=[q_spec, kt_spec, kv_spec, vec(DIFF_HD), vec(DIFF_HD), vec(DIFF_HD), vec(DIFF_HD),
                      vec(DIFF_VD)],
            out_specs=q_spec,
            scratch_shapes=[pltpu.VMEM((2, blk, LANES), F32), pltpu.VMEM((2, blk, LANES), F32),
                            pltpu.VMEM((2, blk, DIFF_VD), F32)]),
        out_shape=jax.ShapeDtypeStruct((batch * seq, DIFF_HEADS * DIFF_VD), BF16),
        compiler_params=_params("parallel", "parallel", "arbitrary"),
    )(qi_tab, ki_tab, dq, dk, dv, *lam_params, g_diff_norm)


def _merge_kernel(x_ref, og_ref, od_ref, g1_ref, wgate_ref, bgate_ref, wbra_ref, wbrb_ref, wo_ref, g2_ref,
                  wxq_ref, x1_ref, xq_ref):
    x = x_ref[...]
    u = _rms(x, g1_ref[...]).astype(BF16)
    gates = _sigmoid(_dot(u, wgate_ref[...]) + bgate_ref[...])
    merged = (gates[:, :D_MODEL] * _dot(og_ref[...], wbra_ref[...])
              + gates[:, D_MODEL:] * _dot(od_ref[...], wbrb_ref[...]))
    x1 = x + _dot(merged.astype(BF16), wo_ref[...])
    x1_ref[...] = x1
    xq_ref[...] = _dot(_rms(x1, g2_ref[...]).astype(BF16), wxq_ref[...]).astype(BF16)


def _merge(x, og, od, g1, w_gate, b_gate, w_br_a, w_br_b, w_o, g2, w_xq, tm):
    m = x.shape[0]
    tok = lambda w: pl.BlockSpec((tm, w), lambda i: (i, 0))
    consts = [g1, w_gate, b_gate, w_br_a, w_br_b, w_o, g2, w_xq]
    return pl.pallas_call(
        _merge_kernel,
        grid=(m // tm,),
        in_specs=[tok(D_MODEL), tok(512), tok(512)] + [_const_spec(c.shape) for c in consts],
        out_specs=[tok(D_MODEL), tok(512)],
        out_shape=[jax.ShapeDtypeStruct((m, D_MODEL), F32), jax.ShapeDtypeStruct((m, 512), BF16)],
        compiler_params=_params("parallel"),
    )(x, og, od, *consts)


def _post_kernel(*refs, shared_memory):
    if shared_memory:
        (x1_ref, xq_ref, mk_ref, mv_ref, wxo_ref, g3_ref, wrh_ref, wrl_ref, br_ref,
         x2_ref, h3_ref, ti_ref, tg_ref) = refs
        xq = xq_ref[...]
        heads = []
        for h in range(X_HEADS):
            sl = slice(h * X_HD, (h + 1) * X_HD)
            s = _dot_nt(xq[:, sl], mk_ref[:, sl]) * (X_HD ** -0.5)
            p = jnp.exp(s - jnp.max(s, axis=-1, keepdims=True))
            pr = p / jnp.sum(p, axis=-1, keepdims=True)
            heads.append(_dot(pr.astype(BF16), mv_ref[:, sl]).astype(BF16))
        o = jnp.concatenate(heads, axis=1)
    else:
        (x1_ref, o_ref, wxo_ref, g3_ref, wrh_ref, wrl_ref, br_ref, x2_ref, h3_ref, ti_ref, tg_ref) = refs
        o = o_ref[...]
    x2 = x1_ref[...] + _dot(o, wxo_ref[...])
    x2_ref[...] = x2
    h3 = _rms(x2, g3_ref[...])
    _to_row_tiles(h3_ref, h3, x2.shape[0])
    h_hi = h3.astype(BF16)
    h_lo = (h3 - h_hi.astype(F32)).astype(BF16)
    logits = (_dot(h_hi, wrh_ref[...]) + _dot(h_lo, wrh_ref[...]) + _dot(h_hi, wrl_ref[...])) + br_ref[...]
    lane = lax.broadcasted_iota(jnp.int32, logits.shape, 1)
    logits = jnp.where(lane < N_EXPERTS, logits, -jnp.inf)
    vals, idxs = [], []
    for _ in range(TOP_K):
        top = jnp.max(logits, axis=-1, keepdims=True)
        idx = jnp.min(jnp.where(logits == top, lane, LANES), axis=-1, keepdims=True)
        vals.append(top)
        idxs.append(idx)
        logits = jnp.where(lane == idx, -jnp.inf, logits)
    exps = [jnp.exp(v - vals[0]) for v in vals]
    denom = exps[0] + exps[1] + exps[2] + exps[3]
    ti = jnp.zeros(lane.shape, jnp.int32)
    tg = jnp.zeros(lane.shape, F32)
    for r in range(TOP_K):
        ti = jnp.where(lane == r, idxs[r], ti)
        tg = jnp.where(lane == r, exps[r] / denom, tg)
    ti_ref[...] = ti
    tg_ref[...] = tg


def _post(x1, attn_in, w_xo, g3, wr_hi, wr_lo, b_r, tm, mem=None, seq=None):
    m = x1.shape[0]
    tok = lambda w: pl.BlockSpec((tm, w), lambda i: (i, 0))
    consts = [w_xo, g3, wr_hi, wr_lo, b_r]
    if mem is not None:
        mk, mv = mem
        n_mem = mk.shape[0] // (m // seq)
        tiles_per_seq = seq // tm
        mem_spec = pl.BlockSpec((n_mem, 512), lambda i: (i // tiles_per_seq, 0))
        in_specs = [tok(D_MODEL), tok(512), mem_spec, mem_spec]
        args = [x1, attn_in, mk, mv]
    else:
        in_specs = [tok(D_MODEL), tok(512)]
        args = [x1, attn_in]
    return pl.pallas_call(
        functools.partial(_post_kernel, shared_memory=mem is not None),
        grid=(m // tm,),
        in_specs=in_specs + [_const_spec(c.shape) for c in consts],
        out_specs=[tok(D_MODEL), pl.BlockSpec((tm * ROW_TILE, LANES), lambda i: (i, 0)), tok(LANES), tok(LANES)],
        out_shape=[jax.ShapeDtypeStruct((m, D_MODEL), F32), jax.ShapeDtypeStruct((m * ROW_TILE, LANES), F32),
                   jax.ShapeDtypeStruct((m, LANES), jnp.int32), jax.ShapeDtypeStruct((m, LANES), F32)],
        compiler_params=_params("parallel"),
    )(*args, *consts)


def _memkv_kernel(m_ref, g_ref, wk_ref, wv_ref, kf_ref, vf_ref, kb_ref, vb_ref):
    u = _rms(m_ref[...], g_ref[...]).astype(BF16)
    mk = _dot(u, wk_ref[...])
    mv = _dot(u, wv_ref[...])
    kf_ref[...] = mk
    vf_ref[...] = mv
    kb_ref[...] = mk.astype(BF16)
    vb_ref[...] = mv.astype(BF16)


def _memkv(mem, g_mem, w_xk, w_xv):
    m = mem.shape[0]
    tm = math.gcd(m, 256)
    tok = lambda w: pl.BlockSpec((tm, w), lambda i: (i, 0))
    return pl.pallas_call(
        _memkv_kernel,
        grid=(m // tm,),
        in_specs=[tok(D_MODEL), _const_spec((1, D_MODEL)), _const_spec(w_xk.shape), _const_spec(w_xv.shape)],
        out_specs=[tok(512)] * 4,
        out_shape=[jax.ShapeDtypeStruct((m, 512), F32)] * 2 + [jax.ShapeDtypeStruct((m, 512), BF16)] * 2,
        compiler_params=_params("parallel"),
    )(mem, g_mem, w_xk, w_xv)


def _row_copy(src_ref, src_row, dst_ref, dst_row, sem):
    src = src_ref.at[pl.ds(pl.multiple_of(src_row * ROW_TILE, ROW_TILE), ROW_TILE), :]
    dst = dst_ref.at[pl.ds(pl.multiple_of(dst_row * ROW_TILE, ROW_TILE), ROW_TILE), :]
    return pltpu.make_async_copy(src, dst, sem)


def _to_row_tiles(ref, value, rows):
    for c in range(D_MODEL // LANES):
        ref[pl.ds(c, rows, stride=ROW_TILE), :] = value[:, c * LANES:(c + 1) * LANES]


def _from_row_tiles(ref, first_row, rows):
    return jnp.concatenate([ref[pl.ds(first_row * ROW_TILE + c, rows, stride=ROW_TILE), :]
                            for c in range(D_MODEL // LANES)], axis=1)


def _dispatch_kernel(slot_ref, h_ref, xs_in_ref, xs_ref, sem, *, tokens):
    del xs_in_ref

    def copies(n):
        return [_row_copy(h_ref, n, xs_ref, slot_ref[0, 0, n * TOP_K + r], sem) for r in range(TOP_K)]

    def issue(n, carry):
        for r, cp in enumerate(copies(n)):
            cp.start(priority=r % 2)
        return carry

    def drain(n, carry):
        for cp in copies(n):
            cp.wait()
        return carry

    lax.fori_loop(0, tokens, issue, 0)
    lax.fori_loop(0, tokens, drain, 0)


def _dispatch(slot, h_tiles, xs, td):
    m = slot.shape[0]
    return pl.pallas_call(
        functools.partial(_dispatch_kernel, tokens=td),
        grid=(m // td,),
        in_specs=[pl.BlockSpec((1, 1, td * TOP_K), lambda i: (i, 0, 0), memory_space=pltpu.SMEM),
                  pl.BlockSpec((td * ROW_TILE, LANES), lambda i: (i, 0)),
                  pl.BlockSpec(memory_space=pl.ANY)],
        out_specs=pl.BlockSpec(memory_space=pl.ANY),
        out_shape=jax.ShapeDtypeStruct(xs.shape, xs.dtype),
        scratch_shapes=[pltpu.SemaphoreType.DMA(())],
        input_output_aliases={2: 0},
        compiler_params=_params("arbitrary"),
    )(slot.reshape(m // td, 1, td * TOP_K), h_tiles, xs)


def _expert_kernel(blk_e, n_used, x_ref, wgu_ref, bgu_ref, wdn_ref, bdn_ref, y_ref, wgu_scr, wdn_scr, *, bm):
    i = pl.program_id(0)
    used = i < n_used[0]
    new_expert = jnp.logical_or(i == 0, blk_e[i] != blk_e[jnp.maximum(i - 1, 0)])

    @pl.when(jnp.logical_and(used, new_expert))
    def _():
        wdn_scr[...] = wdn_ref[0].astype(BF16)
        group = 2 * LANES
        r = lax.broadcasted_iota(jnp.int32, (group, group), 0)
        c = lax.broadcasted_iota(jnp.int32, (group, group), 1)
        perm = jnp.where(r == 2 * (c % LANES) + c // LANES, 1.0, 0.0).astype(BF16)
        for g in range(2 * D_FF // group):
            w = _dot(wgu_ref[0, :, g * group:(g + 1) * group].astype(BF16), perm).astype(BF16)
            wgu_scr[:, g * LANES:(g + 1) * LANES] = w[:, :LANES]
            wgu_scr[:, D_FF + g * LANES:D_FF + (g + 1) * LANES] = w[:, LANES:]

    @pl.when(used)
    def _():
        x = _from_row_tiles(x_ref, 0, bm).astype(BF16)
        h = _dot(x, wgu_scr[...]) + bgu_ref[0]
        glu = jnp.minimum(h[:, :D_FF], SWIGLU_LIMIT)
        lin = jnp.clip(h[:, D_FF:], -SWIGLU_LIMIT, SWIGLU_LIMIT)
        act = (lin + 1.0) * glu * _sigmoid(SWIGLU_ALPHA * glu)
        _to_row_tiles(y_ref, _dot(act.astype(BF16), wdn_scr[...]) + bdn_ref[0], bm)

    @pl.when(jnp.logical_not(used))
    def _():
        y_ref[...] = jnp.zeros_like(y_ref)


def _experts(xs, blk_e, n_used, wgu, bgu, wdn, bdn, bm):
    n_blk = xs.shape[0] // (bm * ROW_TILE)
    rows = pl.BlockSpec((bm * ROW_TILE, LANES), lambda i, be, nu: (i, 0))
    return pl.pallas_call(
        functools.partial(_expert_kernel, bm=bm),
        grid_spec=pltpu.PrefetchScalarGridSpec(
            num_scalar_prefetch=2,
            grid=(n_blk,),
            in_specs=[rows,
                      pl.BlockSpec((1, D_MODEL, 2 * D_FF), lambda i, be, nu: (be[i], 0, 0)),
                      pl.BlockSpec((1, 1, 2 * D_FF), lambda i, be, nu: (be[i], 0, 0)),
                      pl.BlockSpec((1, D_FF, D_MODEL), lambda i, be, nu: (be[i], 0, 0)),
                      pl.BlockSpec((1, 1, D_MODEL), lambda i, be, nu: (be[i], 0, 0))],
            out_specs=rows,
            scratch_shapes=[pltpu.VMEM((D_MODEL, 2 * D_FF), BF16), pltpu.VMEM((D_FF, D_MODEL), BF16)]),
        out_shape=jax.ShapeDtypeStruct(xs.shape, F32),
        compiler_params=_params("arbitrary"),
    )(blk_e, n_used, xs, wgu, bgu, wdn, bdn)


def _combine_kernel(slot_ref, x2_ref, tg_ref, gf_ref, ys_ref, o_ref, buf, sem, *, tokens):
    def copies(n):
        return [_row_copy(ys_ref, slot_ref[0, 0, n * TOP_K + r], buf, r * tokens + n, sem) for r in range(TOP_K)]

    def issue(n, carry):
        for r, cp in enumerate(copies(n)):
            cp.start(priority=r % 2)
        return carry

    def drain(n, carry):
        for cp in copies(n):
            cp.wait()
        return carry

    lax.fori_loop(0, tokens, issue, 0)
    lax.fori_loop(0, tokens, drain, 0)
    tg = tg_ref[...]
    acc = x2_ref[...]
    for r in range(TOP_K):
        acc = acc + tg[:, r:r + 1] * _from_row_tiles(buf, r * tokens, tokens)
    o_ref[...] = _rms(acc, gf_ref[...])


def _combine(slot, x2, tg, g_final, ys, tc):
    m = x2.shape[0]
    tok = lambda w: pl.BlockSpec((tc, w), lambda i: (i, 0))
    return pl.pallas_call(
        functools.partial(_combine_kernel, tokens=tc),
        grid=(m // tc,),
        in_specs=[pl.BlockSpec((1, 1, tc * TOP_K), lambda i: (i, 0, 0), memory_space=pltpu.SMEM),
                  tok(D_MODEL), tok(LANES), _const_spec((1, D_MODEL)), pl.BlockSpec(memory_space=pl.ANY)],
        out_specs=tok(D_MODEL),
        out_shape=jax.ShapeDtypeStruct((m, D_MODEL), F32),
        scratch_shapes=[pltpu.VMEM((TOP_K * tc * ROW_TILE, LANES), F32), pltpu.SemaphoreType.DMA(())],
        compiler_params=_params("arbitrary"),
    )(slot.reshape(m // tc, 1, tc * TOP_K), x2, tg, g_final, ys)


def _moe(groups, moe_w, g_final, bm):
    wgu, bgu, wdn, bdn = moe_w
    e = jnp.concatenate([g[2][:, :TOP_K] for g in groups], axis=0)
    m = e.shape[0]
    n_asg = m * TOP_K
    onehot = (e[:, :, None] == jnp.arange(N_EXPERTS, dtype=jnp.int32)).astype(jnp.int32)
    per_tok = jnp.sum(onehot, axis=1)
    before = jnp.cumsum(per_tok, axis=0) - per_tok
    counts = jnp.sum(per_tok, axis=0)
    padded = (counts + bm - 1) // bm * bm
    pad_end = jnp.cumsum(padded)
    pad_start = pad_end - padded
    slot = jnp.sum(onehot * (before + pad_start)[:, None, :], axis=-1).astype(jnp.int32)
    n_blk = -(-n_asg // bm) + N_EXPERTS
    blk_start = jnp.arange(n_blk, dtype=jnp.int32) * bm
    blk_e = jnp.minimum(jnp.sum((blk_start[:, None] >= pad_end[None, :]).astype(jnp.int32), axis=1),
                        N_EXPERTS - 1).astype(jnp.int32)
    n_used = (pad_end[-1:] // bm).astype(jnp.int32)
    xs = jnp.zeros((n_blk * bm * ROW_TILE, LANES), F32)
    slots, start = [], 0
    for x2, h_tiles, _, _, tile in groups:
        slots.append(slot[start:start + x2.shape[0]])
        start += x2.shape[0]
        xs = _dispatch(slots[-1], h_tiles, xs, tile)
    ys = _experts(xs, blk_e, n_used, wgu, bgu, wdn, bdn, bm)
    return [_combine(s, x2, tg, g_final, ys, tile) for s, (x2, _, _, tg, tile) in zip(slots, groups)]


def _gla_step_kernel(s0_ref, dcol_ref, kcol_ref, qcol_ref, v_ref, r_ref, gn_ref, s_ref, o_ref):
    s_new = jnp.exp(dcol_ref[...]) * s0_ref[...] + kcol_ref[...] * v_ref[...]
    s_ref[...] = s_new
    o = jnp.sum(qcol_ref[...] * s_new, axis=2)
    r = r_ref[...].astype(F32)
    o_ref[...] = (_rms(o, gn_ref[...]) * (r * _sigmoid(r))).astype(BF16)


def _gla_step(s0, gq, gk, gg, gv, gr, g_gla_norm):
    db = s0.shape[0]
    tb = math.gcd(db, 8)
    col = lambda a: jnp.broadcast_to(a.reshape(db, GLA_HEADS, GLA_DK, 1), (db, GLA_HEADS, GLA_DK, GLA_DV))
    big = pl.BlockSpec((tb, GLA_HEADS, GLA_DK, GLA_DV), lambda i: (i, 0, 0, 0))
    return pl.pallas_call(
        _gla_step_kernel,
        grid=(db // tb,),
        in_specs=[big, big, big, big,
                  pl.BlockSpec((tb, GLA_HEADS, 1, GLA_DV), lambda i: (i, 0, 0, 0)),
                  pl.BlockSpec((tb, GLA_HEADS, GLA_DV), lambda i: (i, 0, 0)),
                  pl.BlockSpec((1, 1, GLA_DV), lambda i: (0, 0, 0))],
        out_specs=[big, pl.BlockSpec((tb, GLA_HEADS, GLA_DV), lambda i: (i, 0, 0))],
        out_shape=[jax.ShapeDtypeStruct(s0.shape, F32), jax.ShapeDtypeStruct((db, GLA_HEADS, GLA_DV), BF16)],
        compiler_params=_params("parallel"),
    )(s0, col(gg), col(gk), col(gq), gv.reshape(db, GLA_HEADS, 1, GLA_DV), gr.reshape(db, GLA_HEADS, GLA_DV),
      g_gla_norm.reshape(1, 1, GLA_DV))


def _paged_kernel(pt_ref, q_ref, kn_ref, vn_ref, lq1_ref, lk1_ref, lq2_ref, lk2_ref, gdn_ref, kc_ref, vc_ref,
                  o_ref, kbuf, vbuf, sem, qm_scr, m_scr, l_scr, acc_scr, *, pages):
    seq = pl.program_id(0)
    step = pl.program_id(1)
    n_steps = pl.num_programs(1)
    lin = seq * n_steps + step
    slot = lin % 2
    width = DIFF_HEADS * 2 * DIFF_HD
    n_maps = 2 * DIFF_HEADS

    def page_copies(lin_idx, slot_idx):
        b = lin_idx // n_steps
        first = (lin_idx % n_steps) * pages
        out = []
        for i in range(pages):
            page = pt_ref[b, first + i]
            out.append(pltpu.make_async_copy(kc_ref.at[page], kbuf.at[slot_idx, i], sem.at[0, slot_idx]))
            out.append(pltpu.make_async_copy(vc_ref.at[page], vbuf.at[slot_idx, i], sem.at[1, slot_idx]))
        return out

    @pl.when(lin == 0)
    def _():
        for cp in page_copies(lin, slot):
            cp.start()

    @pl.when(lin + 1 < pl.num_programs(0) * n_steps)
    def _():
        for cp in page_copies(lin + 1, 1 - slot):
            cp.start()

    for cp in page_copies(lin, slot):
        cp.wait()
    k_refs = [kbuf.at[slot, i] for i in range(pages)]
    v_refs = [vbuf.at[slot, i] for i in range(pages)]

    @pl.when(step == 0)
    def _():
        row = lax.broadcasted_iota(jnp.int32, (n_maps, width), 0)
        lane = lax.broadcasted_iota(jnp.int32, (n_maps, width), 1)
        q_rows = jnp.where(lane // DIFF_HD == row, jnp.broadcast_to(q_ref[0], (n_maps, width)), 0.0)
        qm_scr[...] = q_rows
        m_scr[...] = jnp.sum(q_rows * kn_ref[0], axis=-1, keepdims=True)
        l_scr[...] = jnp.ones_like(l_scr)
        acc_scr[...] = jnp.broadcast_to(vn_ref[0], acc_scr.shape)

    qm = qm_scr[...].astype(BF16)
    s = jnp.concatenate([_dot(qm, k_refs[i][...].astype(BF16)) for i in range(pages)], axis=1)
    m_prev = m_scr[...]
    m_new = jnp.maximum(m_prev, jnp.max(s, axis=-1, keepdims=True))
    alpha = jnp.exp2(m_prev - m_new)
    p = jnp.exp2(s - m_new)
    l_scr[...] = alpha * l_scr[...] + jnp.sum(p, axis=-1, keepdims=True)
    acc = alpha * acc_scr[...]
    for i in range(pages):
        p_i = p[:, i * PAGE_SIZE:(i + 1) * PAGE_SIZE].astype(BF16)
        acc = acc + jnp.concatenate(
            [_dot(p_i, v_refs[i][pl.ds(h, PAGE_SIZE, stride=DIFF_HEADS), :].astype(BF16))
             for h in range(DIFF_HEADS)], axis=1)
    acc_scr[...] = acc
    m_scr[...] = m_new

    @pl.when(step == pl.num_programs(1) - 1)
    def _():
        o = acc_scr[...] / l_scr[...]
        for h in range(DIFF_HEADS):
            sl = slice(h * DIFF_VD, (h + 1) * DIFF_VD)
            od = _diff_combine(o[2 * h:2 * h + 1, sl], o[2 * h + 1:2 * h + 2, sl],
                               lq1_ref, lk1_ref, lq2_ref, lk2_ref, gdn_ref)
            o_ref[0, :, sl] = od.astype(BF16)


def _diff_attn_paged(dq, dk, dv, cache_kt, cache_v, page_table, lam_params, g_diff_norm):
    db, n_pages = page_table.shape
    pages = math.gcd(n_pages, 16)
    width = DIFF_HEADS * 2 * DIFF_HD
    row = pl.BlockSpec((1, 1, width), lambda b, s, pt: (b, 0, 0))
    vec = lambda w: pl.BlockSpec((1, w), lambda b, s, pt: (0, 0))

    n_maps = 2 * DIFF_HEADS
    hbm = pl.BlockSpec(memory_space=pl.ANY)
    return pl.pallas_call(
        functools.partial(_paged_kernel, pages=pages),
        grid_spec=pltpu.PrefetchScalarGridSpec(
            num_scalar_prefetch=1,
            grid=(db, n_pages // pages),
            in_specs=[row, row, row, vec(DIFF_HD), vec(DIFF_HD), vec(DIFF_HD), vec(DIFF_HD), vec(DIFF_VD),
                      hbm, hbm],
            out_specs=row,
            scratch_shapes=[pltpu.VMEM((2, pages, width, PAGE_SIZE), F32),
                            pltpu.VMEM((2, pages, PAGE_SIZE * DIFF_HEADS, DIFF_VD), F32),
                            pltpu.SemaphoreType.DMA((2, 2)),
                            pltpu.VMEM((n_maps, width), F32), pltpu.VMEM((n_maps, 1), F32),
                            pltpu.VMEM((n_maps, 1), F32), pltpu.VMEM((n_maps, width), F32)]),
        out_shape=jax.ShapeDtypeStruct((db, 1, width), BF16),
        compiler_params=_params("arbitrary", "arbitrary"),
    )(page_table, dq.reshape(db, 1, width), dk.reshape(db, 1, width), dv.reshape(db, 1, width), *lam_params,
      g_diff_norm, cache_kt, cache_v)


def _cross_sample_kernel(q_ref, mk_ref, mv_ref, o_ref, *, n_mem):
    q = q_ref[0].astype(F32)
    for h in range(X_HEADS):
        sl = slice(h * X_HD, (h + 1) * X_HD)
        q_h = jnp.broadcast_to(q[:, sl], (8, X_HD)).astype(BF16)
        k_h = mk_ref[0, pl.ds(h, n_mem, stride=X_HEADS), :].astype(BF16)
        v_h = mv_ref[0, pl.ds(h, n_mem, stride=X_HEADS), :].astype(BF16)
        s = _dot_nt(q_h, k_h) * (X_HD ** -0.5)
        p = jnp.exp(s - jnp.max(s, axis=-1, keepdims=True))
        pr = p / jnp.sum(p, axis=-1, keepdims=True)
        o_ref[0, :, sl] = _dot(pr.astype(BF16), v_h)[0:1, :].astype(BF16)


def _cross_sample(xq, mem_k, mem_v):
    db, rows, _ = mem_k.shape
    width = X_HEADS * X_HD
    row = pl.BlockSpec((1, 1, width), lambda b: (b, 0, 0))
    mem = pl.BlockSpec((1, rows, X_HD), lambda b: (b, 0, 0))
    return pl.pallas_call(
        functools.partial(_cross_sample_kernel, n_mem=rows // X_HEADS),
        grid=(db,),
        in_specs=[row, mem, mem],
        out_specs=row,
        out_shape=jax.ShapeDtypeStruct((db, 1, width), BF16),
        compiler_params=_params("parallel"),
    )(xq.reshape(db, 1, width), mem_k, mem_v)


def _rope_tables(pos):
    half = DIFF_HD // 2
    inv = jnp.exp(-math.log(ROPE_THETA) * jnp.arange(half, dtype=F32) * (2.0 / DIFF_HD))
    ang = pos[:, None] * inv[None, :]
    cos, sin = jnp.cos(ang), jnp.sin(ang)
    cos_l = jnp.concatenate([cos, cos, cos, cos], axis=1)
    sin_l = jnp.concatenate([-sin, sin, -sin, sin], axis=1)
    return cos_l, sin_l, cos.T, sin.T


def _forward(x_prompt, x_sample, cache_k_diff, cache_v_diff, state_gla, cache_mem_k, cache_mem_v, page_table,
             mem_prompt, g_norm1, w_in, w_gla_a2, b_gla_a, g_gla_norm, lambda_q1, lambda_k1, lambda_q2,
             lambda_k2, g_diff_norm, w_br_a, w_br_b, w_gate, b_gate, w_o, g_norm2, g_mem, w_xq, w_xk, w_xv,
             w_xo, g_norm3, w_router, b_router, w_gate_up, b_gate_up, w_down, b_down, g_final):
    b_p, t_p, d = x_prompt.shape
    db, t_s, _ = x_sample.shape
    assert t_s == 1 and d == D_MODEL and w_in.shape[0] == 1
    n_pages = page_table.shape[1]
    n_pool = cache_k_diff.shape[1]
    n_mem = mem_prompt.shape[1]
    row = lambda a: a.reshape(1, -1).astype(F32)

    wi = w_in[0]
    w_main = jnp.concatenate([wi[:, :1536], wi[:, 1552:2064], wi[:, 2576:]], axis=1).astype(BF16)
    w_kt = wi[:, 2064:2576].T.astype(BF16)
    w_ga = jnp.pad(wi[:, 1536:1552], ((0, 0), (0, LANES - GLA_GATE_RANK))).astype(BF16)
    w_a2 = jnp.pad(w_gla_a2[0], ((0, LANES - GLA_GATE_RANK), (0, 0))).astype(BF16)
    g1, g2, g3, gm, gf = row(g_norm1[0]), row(g_norm2[0]), row(g_norm3[0]), row(g_mem[0]), row(g_final)
    lam_params = [row(lambda_q1[0]), row(lambda_k1[0]), row(lambda_q2[0]), row(lambda_k2[0])]
    gdn, ggn = row(g_diff_norm[0]), row(g_gla_norm[0])
    merge_w = [g1, w_gate[0].astype(BF16), row(b_gate[0]), w_br_a[0].astype(BF16), w_br_b[0].astype(BF16),
               w_o[0].astype(BF16), g2, w_xq[0].astype(BF16)]
    wr = jnp.pad(w_router[0], ((0, 0), (0, LANES - N_EXPERTS)))
    wr_hi = wr.astype(BF16)
    wr_lo = (wr - wr_hi.astype(F32)).astype(BF16)
    post_w = [w_xo[0].astype(BF16), g3, wr_hi, wr_lo, row(jnp.pad(b_router[0], (0, LANES - N_EXPERTS)))]
    bgu = b_gate_up[0].reshape(N_EXPERTS, D_FF, 2).transpose(0, 2, 1).reshape(N_EXPERTS, 1, 2 * D_FF)
    moe_w = [w_gate_up[0], bgu.astype(F32), w_down[0], b_down[0].reshape(N_EXPERTS, 1, d)]

    m_p = b_p * t_p
    tm_p = math.gcd(t_p, 512)
    xp = x_prompt.reshape(m_p, d)
    mkf, mvf, mkb, mvb = _memkv(mem_prompt.reshape(b_p * n_mem, d), gm, w_xk[0].astype(BF16),
                                w_xv[0].astype(BF16))
    width = DIFF_HEADS * 2 * DIFF_HD
    gq, gk, gg, gv, gr, dq, dvf, dvb, dkt_f, dkt_b = _proj(
        xp, g1, w_main, w_kt, w_ga, w_a2, row(b_gla_a[0]), _rope_tables(jnp.arange(t_p, dtype=F32)), tm_p, t_p)
    og, gla_p = _gla_prompt(gq, gk, gg, gv, gr, ggn, b_p, t_p)
    od = _diff_attn_prompt(dq, dkt_b, dvb, lam_params, gdn, b_p, t_p)
    k_prompt = dkt_f.reshape(b_p, DIFF_HEADS, 2, DIFF_HD, t_p).transpose(0, 4, 1, 2, 3)
    x1, xq = _merge(xp, og, od, *merge_w, tm_p)
    x2, h3, ti, tg = _post(x1, xq, *post_w, tm_p, mem=(mkb, mvb), seq=t_p)

    tm_s = db
    xs = x_sample.reshape(db, d)
    past = n_pages * PAGE_SIZE
    sq, sk, sg, sv, sr, sdq, sdvf, _, sdkt_f, _ = _proj(
        xs, g1, w_main, w_kt, w_ga, w_a2, row(b_gla_a[0]), _rope_tables(jnp.full((db,), past, dtype=F32)), tm_s, db)
    sdkf = sdkt_f[0].T
    gla_s, og_s = _gla_step(state_gla[0], sq, sk, sg, sv, sr, ggn)
    cache_kt = jnp.transpose(cache_k_diff[0], (0, 2, 3, 4, 1)).reshape(n_pool, width, PAGE_SIZE)
    cache_v2 = cache_v_diff[0].reshape(n_pool, PAGE_SIZE * DIFF_HEADS, DIFF_VD)
    od_s = _diff_attn_paged(sdq.astype(F32), sdkf, sdvf, cache_kt, cache_v2, page_table, lam_params, gdn)
    x1s, xqs = _merge(xs, og_s.reshape(db, 512), od_s.reshape(db, 512), *merge_w, tm_s)
    o_s = _cross_sample(xqs, cache_mem_k[0].reshape(db, n_mem * X_HEADS, X_HD),
                        cache_mem_v[0].reshape(db, n_mem * X_HEADS, X_HD))
    x2s, h3s, tis, tgs = _post(x1s, o_s.reshape(db, 512), *post_w, tm_s)
    y_p, y_s = _moe([(x2, h3, ti, tg, math.gcd(m_p, 256)), (x2s, h3s, tis, tgs, db)], moe_w, gf, 256)

    return (y_p.reshape(b_p, t_p, d), y_s.reshape(db, 1, d),
            k_prompt[None], dvf.reshape(1, b_p, t_p, DIFF_HEADS, DIFF_VD),
            gla_p[None], mkf.reshape(1, b_p, n_mem, X_HEADS, X_HD), mvf.reshape(1, b_p, n_mem, X_HEADS, X_HD),
            sdkf.reshape(1, db, 1, DIFF_HEADS, 2, DIFF_HD), sdvf.reshape(1, db, 1, DIFF_HEADS, DIFF_VD),
            gla_s[None])


def kernel(x_prompt, x_sample, cache_k_diff, cache_v_diff, state_gla, cache_mem_k, cache_mem_v, page_table, mem_prompt, g_norm1, w_in, w_gla_a2, b_gla_a, g_gla_norm, lambda_q1, lambda_k1, lambda_q2, lambda_k2, g_diff_norm, w_br_a, w_br_b, w_gate, b_gate, w_o, g_norm2, g_mem, w_xq, w_xk, w_xv, w_xo, g_norm3, w_router, b_router, w_gate_up, b_gate_up, w_down, b_down, g_final):
    return _forward(x_prompt, x_sample, cache_k_diff, cache_v_diff, state_gla, cache_mem_k, cache_mem_v,
                    page_table, mem_prompt, g_norm1, w_in, w_gla_a2, b_gla_a, g_gla_norm, lambda_q1, lambda_k1,
                    lambda_q2, lambda_k2, g_diff_norm, w_br_a, w_br_b, w_gate, b_gate, w_o, g_norm2, g_mem,
                    w_xq, w_xk, w_xv, w_xo, g_norm3, w_router, b_router, w_gate_up, b_gate_up, w_down, b_down,
                    g_final)
```

```python
import functools
import math

import jax
import jax.numpy as jnp
from jax import lax
from jax.experimental import pallas as pl
from jax.experimental.pallas import tpu as pltpu

F32 = jnp.float32
BF16 = jnp.bfloat16

D_MODEL = 1024
GLA_HEADS = 4
GLA_DK = 64
GLA_DV = 128
GLA_GATE_RANK = 16
GLA_GATE_NORM = 16.0
DIFF_HEADS = 4
DIFF_HD = 64
DIFF_VD = 128
X_HEADS = 4
X_HD = 128
N_EXPERTS = 32
TOP_K = 4
D_FF = 1024
SWIGLU_LIMIT = 7.0
SWIGLU_ALPHA = 1.702
PAGE_SIZE = 128
ROPE_THETA = 10000.0
EPS = 1e-6
LAMBDA_INIT = 0.8 - 0.6 * math.exp(-0.3 * 0)

LANES = 128
ROW_TILE = 8
LOG2E = math.log2(math.e)
GLA_CHUNK = 64
GLA_SUB = 16
GLA_EXP_CLAMP = 80.0
VMEM_LIMIT = 56 * 1024 * 1024


def _dot(a, b):
    return jnp.dot(a, b, preferred_element_type=F32)


def _dot_nt(a, b):
    return lax.dot_general(a, b, (((1,), (1,)), ((), ())), preferred_element_type=F32)


def _dot_tn(a, b):
    return lax.dot_general(a, b, (((0,), (0,)), ((), ())), preferred_element_type=F32)


def _rms(x, g):
    return x * lax.rsqrt(jnp.mean(x * x, axis=-1, keepdims=True) + EPS) * g


def _sigmoid(x):
    return 1.0 / (1.0 + jnp.exp(-x))


def _params(*sem):
    return pltpu.CompilerParams(dimension_semantics=sem, vmem_limit_bytes=VMEM_LIMIT)


def _const_spec(shape):
    nd = len(shape)
    return pl.BlockSpec(shape, lambda *_: (0,) * nd)


def _proj_kernel(x_ref, g1_ref, w_ref, wkt_ref, wga_ref, wa2_ref, ba_ref, cos_ref, sin_ref, cost_ref, sint_ref,
                 gq_ref, gk_ref, gg_ref, gv_ref, gr_ref, dq_ref, dvf_ref, dvb_ref, dkf_ref, dkb_ref):
    x = x_ref[...]
    u = _rms(x, g1_ref[...]).astype(BF16)

    def seg(lo, hi):
        return _dot(u, w_ref[:, lo:hi])

    gq_ref[...] = seg(0, 256) * (GLA_DK ** -0.5)
    gk_ref[...] = seg(256, 512)
    gv_ref[...] = seg(512, 1024).astype(BF16)
    gr_ref[...] = seg(1024, 1536).astype(BF16)
    ga = _dot(u, wga_ref[...])
    z = _dot(ga.astype(BF16), wa2_ref[...]) + ba_ref[...]
    log_sig = jnp.minimum(z, 0.0) - jnp.log(1.0 + jnp.exp(-jnp.abs(z)))
    gg_ref[...] = log_sig * (1.0 / GLA_GATE_NORM)

    cos = jnp.concatenate([cos_ref[...]] * 4, axis=1)
    sin = jnp.concatenate([sin_ref[...]] * 4, axis=1)
    width = 4 * LANES
    lane = lax.broadcasted_iota(jnp.int32, (1, width), 1)
    first_half = (lane % DIFF_HD) < (DIFF_HD // 2)

    def rope(v):
        partner = jnp.where(first_half, pltpu.roll(v, width - DIFF_HD // 2, 1), pltpu.roll(v, DIFF_HD // 2, 1))
        return v * cos + partner * sin

    dq = rope(seg(1536, 2048))
    dq_ref[...] = (dq * (DIFF_HD ** -0.5 * LOG2E)).astype(BF16)
    dv = seg(2048, 2560)
    dvf_ref[...] = dv
    dvb_ref[...] = dv.astype(BF16)
    k_t = _dot_nt(wkt_ref[...], u)
    cos_t = cost_ref[...]
    sin_t = sint_ref[...]
    half = DIFF_HD // 2
    slabs = []
    for grp in range(2 * DIFF_HEADS):
        x1 = k_t[grp * DIFF_HD:grp * DIFF_HD + half, :]
        x2 = k_t[grp * DIFF_HD + half:(grp + 1) * DIFF_HD, :]
        slabs += [x1 * cos_t - x2 * sin_t, x2 * cos_t + x1 * sin_t]
    k_rot = jnp.concatenate(slabs, axis=0)
    dkf_ref[0] = k_rot
    dkb_ref[0] = k_rot.astype(BF16)


def _proj(x, g1, w_main, w_kt, w_ga, w_a2, b_a, rope_tables, tm, seq):
    m = x.shape[0]
    cos, sin, cos_t, sin_t = rope_tables
    n_seq = seq // tm
    width = DIFF_HEADS * 2 * DIFF_HD
    tok = lambda w: pl.BlockSpec((tm, w), lambda i: (i, 0))
    rope = pl.BlockSpec((tm, LANES), lambda i: (i % n_seq, 0))
    rope_t = pl.BlockSpec((DIFF_HD // 2, tm), lambda i: (0, i % n_seq))
    key_t = pl.BlockSpec((1, width, tm), lambda i: (i // n_seq, 0, i % n_seq))
    outs = [(256, F32), (256, F32), (256, F32), (512, BF16), (512, BF16), (512, BF16), (512, F32), (512, BF16)]
    return pl.pallas_call(
        _proj_kernel,
        grid=(m // tm,),
        in_specs=[tok(D_MODEL), _const_spec((1, D_MODEL)), _const_spec(w_main.shape), _const_spec(w_kt.shape),
                  _const_spec(w_ga.shape), _const_spec(w_a2.shape), _const_spec((1, 256)),
                  rope, rope, rope_t, rope_t],
        out_specs=[tok(w) for w, _ in outs] + [key_t, key_t],
        out_shape=[jax.ShapeDtypeStruct((m, w), dt) for w, dt in outs]
        + [jax.ShapeDtypeStruct((m // seq, width, seq), F32), jax.ShapeDtypeStruct((m // seq, width, seq), BF16)],
        compiler_params=_params("parallel"),
    )(x, g1, w_main, w_kt, w_ga, w_a2, b_a, cos, sin, cos_t, sin_t)


def _gla_kernel(q_ref, k_ref, g_ref, v_ref, r_ref, gn_ref, o_ref, s_ref, st_scr, *, chunk, sub, seqs):
    c = pl.program_id(1)

    @pl.when(c == 0)
    def _():
        st_scr[...] = jnp.zeros_like(st_scr)

    for sq in range(seqs):
        _gla_chunk(sq, q_ref, k_ref, g_ref, v_ref, r_ref, gn_ref, o_ref, st_scr, chunk, sub)

    @pl.when(c == pl.num_programs(1) - 1)
    def _():
        for sq in range(seqs):
            for pair in range(GLA_HEADS // 2):
                s_ref[sq, 2 * pair:2 * pair + 2] = st_scr[sq, pair].T.reshape(2, GLA_DK, GLA_DV)


def _gla_chunk(sq, q_ref, k_ref, g_ref, v_ref, r_ref, gn_ref, o_ref, st_scr, chunk, sub):
    g = g_ref[sq]
    row = lax.broadcasted_iota(jnp.int32, (chunk, chunk), 0)
    col = lax.broadcasted_iota(jnp.int32, (chunk, chunk), 1)
    causal = col <= row
    tril = jnp.where(causal, 1.0, 0.0).astype(BF16)
    g_hi = g.astype(BF16)
    r1 = g - g_hi.astype(F32)
    g_mid = r1.astype(BF16)
    g_lo = (r1 - g_mid.astype(F32)).astype(BF16)
    cum = _dot(tril, g_hi) + _dot(tril, g_mid) + _dot(tril, g_lo)
    q = q_ref[sq]
    k = k_ref[sq]
    last = cum[chunk - 1:chunk, :]
    q_dec = q * jnp.exp(cum)
    k_dec = k * jnp.exp(last - cum)
    lane = lax.broadcasted_iota(jnp.int32, (1, LANES), 1)
    head_mask = (lane < GLA_DK, lane >= GLA_DK)
    gn = gn_ref[...]

    for pair in range(GLA_HEADS // 2):
        sl = slice(pair * LANES, (pair + 1) * LANES)
        state_t = st_scr[sq, pair]
        cum_p, q_p, k_p = cum[:, sl], q[:, sl], k[:, sl]
        q_dec_p, k_dec_p = q_dec[:, sl], k_dec[:, sl]
        k_anchor = []
        for i in range(chunk // sub):
            anchor = cum_p[i * sub:i * sub + 1, :]
            k_anchor.append((k_p * jnp.exp(jnp.minimum(anchor - cum_p, GLA_EXP_CLAMP))).astype(BF16))
        state_new = jnp.exp(last[:, sl]) * state_t
        state_bf = state_t.astype(BF16)
        for hh in range(2):
            h = 2 * pair + hh
            hm = head_mask[hh]
            v_h = v_ref[sq, :, h * GLA_DV:(h + 1) * GLA_DV]
            rows = []
            for i in range(chunk // sub):
                r0 = i * sub
                q_a = q_p[r0:r0 + sub, :] * jnp.exp(cum_p[r0:r0 + sub, :] - cum_p[r0:r0 + 1, :])
                rows.append(_dot_nt(jnp.where(hm, q_a, 0.0).astype(BF16), k_anchor[i]))
            a = jnp.where(causal, jnp.concatenate(rows, axis=0), 0.0).astype(BF16)
            o = _dot_nt(jnp.where(hm, q_dec_p, 0.0).astype(BF16), state_bf) + _dot(a, v_h)
            state_new = state_new + _dot_tn(v_h, jnp.where(hm, k_dec_p, 0.0).astype(BF16))
            r = r_ref[sq, :, h * GLA_DV:(h + 1) * GLA_DV].astype(F32)
            o_ref[sq, :, h * GLA_DV:(h + 1) * GLA_DV] = (_rms(o, gn) * (r * _sigmoid(r))).astype(BF16)
        st_scr[sq, pair] = state_new


def _gla_prompt(gq, gk, gg, gv, gr, g_gla_norm, batch, seq):
    chunk = math.gcd(seq, GLA_CHUNK)
    sub = math.gcd(chunk, GLA_SUB)
    seqs = math.gcd(batch, 4)
    tok = lambda w: pl.BlockSpec((seqs, chunk, w), lambda b, c: (b, c, 0))
    per_seq = lambda a: a.reshape(batch, seq, a.shape[-1])
    og, state = pl.pallas_call(
        functools.partial(_gla_kernel, chunk=chunk, sub=sub, seqs=seqs),
        grid=(batch // seqs, seq // chunk),
        in_specs=[tok(256), tok(256), tok(256), tok(512), tok(512), _const_spec((1, GLA_DV))],
        out_specs=[tok(512), pl.BlockSpec((seqs, GLA_HEADS, GLA_DK, GLA_DV), lambda b, c: (b, 0, 0, 0))],
        out_shape=[jax.ShapeDtypeStruct((batch, seq, 512), BF16),
                   jax.ShapeDtypeStruct((batch, GLA_HEADS, GLA_DK, GLA_DV), F32)],
        scratch_shapes=[pltpu.VMEM((seqs, GLA_HEADS // 2, GLA_DV, 2 * GLA_DK), F32)],
        compiler_params=_params("parallel", "arbitrary"),
    )(per_seq(gq), per_seq(gk), per_seq(gg), per_seq(gv), per_seq(gr), g_gla_norm)
    return og.reshape(batch * seq, 512), state


def _diff_combine(o1, o2, lq1_ref, lk1_ref, lq2_ref, lk2_ref, gdn_ref):
    lam = (jnp.exp(jnp.sum(lq1_ref[...] * lk1_ref[...], axis=-1, keepdims=True))
           - jnp.exp(jnp.sum(lq2_ref[...] * lk2_ref[...], axis=-1, keepdims=True)) + LAMBDA_INIT)
    od = o1 - lam * o2
    return _rms(od, gdn_ref[...]) * (1.0 - LAMBDA_INIT)


def _attn_kernel(qi_tab, ki_tab, q_ref, k_ref, v_ref, lq1_ref, lk1_ref, lq2_ref, lk2_ref, gdn_ref,
                 o_ref, m_scr, acc_scr, *, blk):
    step = pl.program_id(2)
    qi = qi_tab[step]
    ki = ki_tab[step]

    @pl.when(ki == 0)
    def _():
        m_scr[...] = jnp.full_like(m_scr, -jnp.inf)
        acc_scr[...] = jnp.zeros_like(acc_scr)

    lane = lax.broadcasted_iota(jnp.int32, (1, LANES), 1)
    map_mask = (lane < DIFF_HD, lane >= DIFF_HD)

    def update(diagonal):
        q = q_ref[...]
        k_t = k_ref[0]
        v = v_ref[...]
        if diagonal:
            row = lax.broadcasted_iota(jnp.int32, (blk, blk), 0)
            col = lax.broadcasted_iota(jnp.int32, (blk, blk), 1)
            keep = col <= row
        v_ext = jnp.concatenate([v, jnp.ones_like(v)], axis=1)
        scores = [_dot(jnp.where(map_mask[j], q, jnp.zeros_like(q)), k_t) for j in range(2)]
        for j in range(2):
            s = scores[j]
            if diagonal:
                s = jnp.where(keep, s, -jnp.inf)
            m_prev = m_scr[j]
            m_new = jnp.maximum(m_prev, jnp.max(s, axis=-1, keepdims=True))
            alpha = jnp.exp2(m_prev - m_new)
            p = jnp.exp2((s - jnp.concatenate([m_new] * (blk // LANES), axis=1)).astype(BF16))
            acc_scr[j] = jnp.concatenate([alpha, alpha], axis=1) * acc_scr[j] + _dot(p, v_ext)
            m_scr[j] = m_new

    @pl.when(ki < qi)
    def _():
        update(False)

    @pl.when(ki == qi)
    def _():
        update(True)
        o1 = acc_scr[0, :, :DIFF_VD] / acc_scr[0, :, DIFF_VD:]
        o2 = acc_scr[1, :, :DIFF_VD] / acc_scr[1, :, DIFF_VD:]
        o_ref[...] = _diff_combine(o1, o2, lq1_ref, lk1_ref, lq2_ref, lk2_ref, gdn_ref).astype(BF16)


def _diff_attn_prompt(dq, dk, dv, lam_params, g_diff_norm, batch, seq):
    blk = math.gcd(seq, 512)
    nb = seq // blk
    pairs = [(qi, ki) for qi in range(nb) for ki in range(qi + 1)]
    qi_tab = jnp.array([p[0] for p in pairs], jnp.int32)
    ki_tab = jnp.array([p[1] for p in pairs], jnp.int32)
    q_spec = pl.BlockSpec((blk, LANES), lambda b, h, s, qt, kt: (b * nb + qt[s], h))
    kv_spec = pl.BlockSpec((blk, LANES), lambda b, h, s, qt, kt: (b * nb + kt[s], h))
    kt_spec = pl.BlockSpec((1, LANES, blk), lambda b, h, s, qt, kt: (b, h, kt[s]))
    vec = lambda w: pl.BlockSpec((1, w), lambda b, h, s, qt, kt: (0, 0))
    return pl.pallas_call(
        functools.partial(_attn_kernel, blk=blk),
        grid_spec=pltpu.PrefetchScalarGridSpec(
            num_scalar_prefetch=2,
            grid=(batch, DIFF_HEADS, len(pairs)),
            in_specs=[q_spec, kt_spec, kv_spec, vec(DIFF_HD), vec(DIFF_HD), vec(DIFF_HD), vec(DIFF_HD),
                      vec(DIFF_VD)],
            out_specs=q_spec,
            scratch_shapes=[pltpu.VMEM((2, blk, LANES), F32), pltpu.VMEM((2, blk, 2 * DIFF_VD), F32)]),
        out_shape=jax.ShapeDtypeStruct((batch * seq, DIFF_HEADS * DIFF_VD), BF16),
        compiler_params=_params("parallel", "parallel", "arbitrary"),
    )(qi_tab, ki_tab, dq, dk, dv, *lam_params, g_diff_norm)


def _merge_kernel(x_ref, og_ref, od_ref, g1_ref, wgate_ref, bgate_ref, wbra_ref, wbrb_ref, wo_ref, g2_ref,
                  wxq_ref, x1_ref, xq_ref):
    x = x_ref[...]
    u = _rms(x, g1_ref[...]).astype(BF16)
    gates = _sigmoid(_dot(u, wgate_ref[...]) + bgate_ref[...])
    merged = (gates[:, :D_MODEL] * _dot(og_ref[...], wbra_ref[...])
              + gates[:, D_MODEL:] * _dot(od_ref[...], wbrb_ref[...]))
    x1 = x + _dot(merged.astype(BF16), wo_ref[...])
    x1_ref[...] = x1
    xq_ref[...] = _dot(_rms(x1, g2_ref[...]).astype(BF16), wxq_ref[...]).astype(BF16)


def _merge(x, og, od, g1, w_gate, b_gate, w_br_a, w_br_b, w_o, g2, w_xq, tm):
    m = x.shape[0]
    tok = lambda w: pl.BlockSpec((tm, w), lambda i: (i, 0))
    consts = [g1, w_gate, b_gate, w_br_a, w_br_b, w_o, g2, w_xq]
    return pl.pallas_call(
        _merge_kernel,
        grid=(m // tm,),
        in_specs=[tok(D_MODEL), tok(512), tok(512)] + [_const_spec(c.shape) for c in consts],
        out_specs=[tok(D_MODEL), tok(512)],
        out_shape=[jax.ShapeDtypeStruct((m, D_MODEL), F32), jax.ShapeDtypeStruct((m, 512), BF16)],
        compiler_params=_params("parallel"),
    )(x, og, od, *consts)


def _post_kernel(*refs, shared_memory):
    if shared_memory:
        (x1_ref, xq_ref, mk_ref, mv_ref, wxo_ref, g3_ref, wrh_ref, wrl_ref, br_ref,
         x2_ref, h3_ref, ti_ref, tg_ref) = refs
        xq = xq_ref[...]
        heads = []
        for h in range(X_HEADS):
            sl = slice(h * X_HD, (h + 1) * X_HD)
            s = _dot_nt(xq[:, sl], mk_ref[:, sl]) * (X_HD ** -0.5)
            p = jnp.exp(s - jnp.max(s, axis=-1, keepdims=True))
            pr = p / jnp.sum(p, axis=-1, keepdims=True)
            heads.append(_dot(pr.astype(BF16), mv_ref[:, sl]).astype(BF16))
        o = jnp.concatenate(heads, axis=1)
    else:
        (x1_ref, o_ref, wxo_ref, g3_ref, wrh_ref, wrl_ref, br_ref, x2_ref, h3_ref, ti_ref, tg_ref) = refs
        o = o_ref[...]
    x2 = x1_ref[...] + _dot(o, wxo_ref[...])
    x2_ref[...] = x2
    h3 = _rms(x2, g3_ref[...])
    _to_row_tiles(h3_ref, h3, x2.shape[0])
    h_hi = h3.astype(BF16)
    h_lo = (h3 - h_hi.astype(F32)).astype(BF16)
    logits = (_dot(h_hi, wrh_ref[...]) + _dot(h_lo, wrh_ref[...]) + _dot(h_hi, wrl_ref[...])) + br_ref[...]
    lane = lax.broadcasted_iota(jnp.int32, logits.shape, 1)
    logits = jnp.where(lane < N_EXPERTS, logits, -jnp.inf)
    vals, idxs = [], []
    for _ in range(TOP_K):
        top = jnp.max(logits, axis=-1, keepdims=True)
        idx = jnp.min(jnp.where(logits == top, lane, LANES), axis=-1, keepdims=True)
        vals.append(top)
        idxs.append(idx)
        logits = jnp.where(lane == idx, -jnp.inf, logits)
    exps = [jnp.exp(v - vals[0]) for v in vals]
    denom = exps[0] + exps[1] + exps[2] + exps[3]
    ti = jnp.zeros(lane.shape, jnp.int32)
    tg = jnp.zeros(lane.shape, F32)
    for r in range(TOP_K):
        ti = jnp.where(lane == r, idxs[r], ti)
        tg = jnp.where(lane == r, exps[r] / denom, tg)
    ti_ref[...] = ti
    tg_ref[...] = tg


def _post(x1, attn_in, w_xo, g3, wr_hi, wr_lo, b_r, tm, mem=None, seq=None):
    m = x1.shape[0]
    tok = lambda w: pl.BlockSpec((tm, w), lambda i: (i, 0))
    consts = [w_xo, g3, wr_hi, wr_lo, b_r]
    if mem is not None:
        mk, mv = mem
        n_mem = mk.shape[0] // (m // seq)
        tiles_per_seq = seq // tm
        mem_spec = pl.BlockSpec((n_mem, 512), lambda i: (i // tiles_per_seq, 0))
        in_specs = [tok(D_MODEL), tok(512), mem_spec, mem_spec]
        args = [x1, attn_in, mk, mv]
    else:
        in_specs = [tok(D_MODEL), tok(512)]
        args = [x1, attn_in]
    return pl.pallas_call(
        functools.partial(_post_kernel, shared_memory=mem is not None),
        grid=(m // tm,),
        in_specs=in_specs + [_const_spec(c.shape) for c in consts],
        out_specs=[tok(D_MODEL), pl.BlockSpec((tm * ROW_TILE, LANES), lambda i: (i, 0)), tok(LANES), tok(LANES)],
        out_shape=[jax.ShapeDtypeStruct((m, D_MODEL), F32), jax.ShapeDtypeStruct((m * ROW_TILE, LANES), F32),
                   jax.ShapeDtypeStruct((m, LANES), jnp.int32), jax.ShapeDtypeStruct((m, LANES), F32)],
        compiler_params=_params("parallel"),
    )(*args, *consts)


def _memkv_kernel(m_ref, g_ref, wk_ref, wv_ref, kf_ref, vf_ref, kb_ref, vb_ref):
    u = _rms(m_ref[...], g_ref[...]).astype(BF16)
    mk = _dot(u, wk_ref[...])
    mv = _dot(u, wv_ref[...])
    kf_ref[...] = mk
    vf_ref[...] = mv
    kb_ref[...] = mk.astype(BF16)
    vb_ref[...] = mv.astype(BF16)


def _memkv(mem, g_mem, w_xk, w_xv):
    m = mem.shape[0]
    tm = math.gcd(m, 256)
    tok = lambda w: pl.BlockSpec((tm, w), lambda i: (i, 0))
    return pl.pallas_call(
        _memkv_kernel,
        grid=(m // tm,),
        in_specs=[tok(D_MODEL), _const_spec((1, D_MODEL)), _const_spec(w_xk.shape), _const_spec(w_xv.shape)],
        out_specs=[tok(512)] * 4,
        out_shape=[jax.ShapeDtypeStruct((m, 512), F32)] * 2 + [jax.ShapeDtypeStruct((m, 512), BF16)] * 2,
        compiler_params=_params("parallel"),
    )(mem, g_mem, w_xk, w_xv)


def _row_copy(src_ref, src_row, dst_ref, dst_row, sem):
    src = src_ref.at[pl.ds(pl.multiple_of(src_row * ROW_TILE, ROW_TILE), ROW_TILE), :]
    dst = dst_ref.at[pl.ds(pl.multiple_of(dst_row * ROW_TILE, ROW_TILE), ROW_TILE), :]
    return pltpu.make_async_copy(src, dst, sem)


def _to_row_tiles(ref, value, rows):
    for c in range(D_MODEL // LANES):
        ref[pl.ds(c, rows, stride=ROW_TILE), :] = value[:, c * LANES:(c + 1) * LANES]


def _from_row_tiles(ref, first_row, rows):
    return jnp.concatenate([ref[pl.ds(first_row * ROW_TILE + c, rows, stride=ROW_TILE), :]
                            for c in range(D_MODEL // LANES)], axis=1)


def _run_row_copies(tokens, copies):
    def issue(n, carry):
        for r, cp in enumerate(copies(n)):
            cp.start(priority=r % 2)
        return carry

    def drain(n, carry):
        for cp in copies(n):
            cp.wait()
        return carry

    unroll = math.gcd(tokens, 8)
    lax.fori_loop(0, tokens, issue, 0, unroll=unroll)
    lax.fori_loop(0, tokens, drain, 0, unroll=unroll)


def _dispatch_kernel(slot_ref, h_ref, xs_in_ref, xs_ref, sem, *, tokens):
    del xs_in_ref

    def copies(n):
        return [_row_copy(h_ref, n, xs_ref, slot_ref[0, 0, n * TOP_K + r], sem) for r in range(TOP_K)]

    _run_row_copies(tokens, copies)


def _dispatch(slot, h_tiles, xs, td):
    m = slot.shape[0]
    return pl.pallas_call(
        functools.partial(_dispatch_kernel, tokens=td),
        grid=(m // td,),
        in_specs=[pl.BlockSpec((1, 1, td * TOP_K), lambda i: (i, 0, 0), memory_space=pltpu.SMEM),
                  pl.BlockSpec((td * ROW_TILE, LANES), lambda i: (i, 0)),
                  pl.BlockSpec(memory_space=pl.ANY)],
        out_specs=pl.BlockSpec(memory_space=pl.ANY),
        out_shape=jax.ShapeDtypeStruct(xs.shape, xs.dtype),
        scratch_shapes=[pltpu.SemaphoreType.DMA(())],
        input_output_aliases={2: 0},
        compiler_params=_params("arbitrary"),
    )(slot.reshape(m // td, 1, td * TOP_K), h_tiles, xs)


def _expert_kernel(blk_e, n_used, next_blk, x_ref, wgu_hbm, bgu_ref, wdn_hbm, bdn_ref, y_ref,
                   wgu_f32, wdn_f32, wgu_scr, wdn_scr, sem, count_ref, *, bm):
    i = pl.program_id(0)
    used = i < n_used[0]
    e = blk_e[i]
    new_expert = jnp.logical_or(i == 0, e != blk_e[jnp.maximum(i - 1, 0)])

    def weight_copies(expert, slot):
        return [pltpu.make_async_copy(wgu_hbm.at[expert], wgu_f32.at[slot], sem.at[0, slot]),
                pltpu.make_async_copy(wdn_hbm.at[expert], wdn_f32.at[slot], sem.at[1, slot])]

    @pl.when(i == 0)
    def _():
        count_ref[0] = 0

    @pl.when(jnp.logical_and(used, new_expert))
    def _():
        slot = count_ref[0] % 2
        count_ref[0] = count_ref[0] + 1

        @pl.when(i == 0)
        def _():
            for cp in weight_copies(e, slot):
                cp.start()

        following = next_blk[e]

        @pl.when(following < n_used[0])
        def _():
            for cp in weight_copies(blk_e[jnp.minimum(following, pl.num_programs(0) - 1)], 1 - slot):
                cp.start()

        for cp in weight_copies(e, slot):
            cp.wait()
        wdn_scr[...] = wdn_f32[slot].astype(BF16)
        group = 2 * LANES
        r = lax.broadcasted_iota(jnp.int32, (group, group), 0)
        c = lax.broadcasted_iota(jnp.int32, (group, group), 1)
        perm = jnp.where(r == 2 * (c % LANES) + c // LANES, 1.0, 0.0).astype(BF16)
        for g in range(2 * D_FF // group):
            w = _dot(wgu_f32[slot, :, g * group:(g + 1) * group].astype(BF16), perm).astype(BF16)
            wgu_scr[:, g * LANES:(g + 1) * LANES] = w[:, :LANES]
            wgu_scr[:, D_FF + g * LANES:D_FF + (g + 1) * LANES] = w[:, LANES:]

    @pl.when(used)
    def _():
        x = _from_row_tiles(x_ref, 0, bm).astype(BF16)
        h = _dot(x, wgu_scr[...]) + bgu_ref[0]
        glu = jnp.minimum(h[:, :D_FF], SWIGLU_LIMIT)
        lin = jnp.clip(h[:, D_FF:], -SWIGLU_LIMIT, SWIGLU_LIMIT)
        act = (lin + 1.0) * glu * _sigmoid(SWIGLU_ALPHA * glu)
        _to_row_tiles(y_ref, _dot(act.astype(BF16), wdn_scr[...]) + bdn_ref[0], bm)

    @pl.when(jnp.logical_not(used))
    def _():
        y_ref[...] = jnp.zeros_like(y_ref)


def _experts(xs, blk_e, n_used, next_blk, wgu, bgu, wdn, bdn, bm):
    n_blk = xs.shape[0] // (bm * ROW_TILE)
    rows = pl.BlockSpec((bm * ROW_TILE, LANES), lambda i, be, nu, nb: (i, 0))
    hbm = pl.BlockSpec(memory_space=pl.ANY)
    return pl.pallas_call(
        functools.partial(_expert_kernel, bm=bm),
        grid_spec=pltpu.PrefetchScalarGridSpec(
            num_scalar_prefetch=3,
            grid=(n_blk,),
            in_specs=[rows, hbm,
                      pl.BlockSpec((1, 1, 2 * D_FF), lambda i, be, nu, nb: (be[i], 0, 0)), hbm,
                      pl.BlockSpec((1, 1, D_MODEL), lambda i, be, nu, nb: (be[i], 0, 0))],
            out_specs=rows,
            scratch_shapes=[pltpu.VMEM((2, D_MODEL, 2 * D_FF), F32), pltpu.VMEM((2, D_FF, D_MODEL), F32),
                            pltpu.VMEM((D_MODEL, 2 * D_FF), BF16), pltpu.VMEM((D_FF, D_MODEL), BF16),
                            pltpu.SemaphoreType.DMA((2, 2)), pltpu.SMEM((1,), jnp.int32)]),
        out_shape=jax.ShapeDtypeStruct(xs.shape, F32),
        compiler_params=_params("arbitrary"),
    )(blk_e, n_used, next_blk, xs, wgu, bgu, wdn, bdn)


def _combine_kernel(slot_ref, x2_ref, tg_ref, gf_ref, ys_ref, o_ref, buf, sem, *, tokens):
    def copies(n):
        return [_row_copy(ys_ref, slot_ref[0, 0, n * TOP_K + r], buf, r * tokens + n, sem) for r in range(TOP_K)]

    _run_row_copies(tokens, copies)
    tg = tg_ref[...]
    acc = x2_ref[...]
    for r in range(TOP_K):
        acc = acc + tg[:, r:r + 1] * _from_row_tiles(buf, r * tokens, tokens)
    o_ref[...] = _rms(acc, gf_ref[...])


def _combine(slot, x2, tg, g_final, ys, tc):
    m = x2.shape[0]
    tok = lambda w: pl.BlockSpec((tc, w), lambda i: (i, 0))
    return pl.pallas_call(
        functools.partial(_combine_kernel, tokens=tc),
        grid=(m // tc,),
        in_specs=[pl.BlockSpec((1, 1, tc * TOP_K), lambda i: (i, 0, 0), memory_space=pltpu.SMEM),
                  tok(D_MODEL), tok(LANES), _const_spec((1, D_MODEL)), pl.BlockSpec(memory_space=pl.ANY)],
        out_specs=tok(D_MODEL),
        out_shape=jax.ShapeDtypeStruct((m, D_MODEL), F32),
        scratch_shapes=[pltpu.VMEM((TOP_K * tc * ROW_TILE, LANES), F32), pltpu.SemaphoreType.DMA(())],
        compiler_params=_params("arbitrary"),
    )(slot.reshape(m // tc, 1, tc * TOP_K), x2, tg, g_final, ys)


def _moe(groups, moe_w, g_final, bm):
    wgu, bgu, wdn, bdn = moe_w
    e = jnp.concatenate([g[2][:, :TOP_K] for g in groups], axis=0)
    m = e.shape[0]
    n_asg = m * TOP_K
    onehot = (e[:, :, None] == jnp.arange(N_EXPERTS, dtype=jnp.int32)).astype(jnp.int32)
    per_tok = jnp.sum(onehot, axis=1)
    before = jnp.cumsum(per_tok, axis=0) - per_tok
    counts = jnp.sum(per_tok, axis=0)
    padded = (counts + bm - 1) // bm * bm
    pad_end = jnp.cumsum(padded)
    pad_start = pad_end - padded
    slot = jnp.sum(onehot * (before + pad_start)[:, None, :], axis=-1).astype(jnp.int32)
    n_blk = -(-n_asg // bm) + N_EXPERTS
    blk_start = jnp.arange(n_blk, dtype=jnp.int32) * bm
    blk_e = jnp.minimum(jnp.sum((blk_start[:, None] >= pad_end[None, :]).astype(jnp.int32), axis=1),
                        N_EXPERTS - 1).astype(jnp.int32)
    n_used = (pad_end[-1:] // bm).astype(jnp.int32)
    xs = jnp.zeros((n_blk * bm * ROW_TILE, LANES), F32)
    slots, start = [], 0
    for x2, h_tiles, _, _, tile in groups:
        slots.append(slot[start:start + x2.shape[0]])
        start += x2.shape[0]
        xs = _dispatch(slots[-1], h_tiles, xs, tile)
    next_blk = (pad_end // bm).astype(jnp.int32)
    ys = _experts(xs, blk_e, n_used, next_blk, wgu, bgu, wdn, bdn, bm)
    return [_combine(s, x2, tg, g_final, ys, tile) for s, (x2, _, _, tg, tile) in zip(slots, groups)]


def _gla_step_kernel(s0_ref, dcol_ref, kcol_ref, qcol_ref, v_ref, r_ref, gn_ref, s_ref, o_ref):
    s_new = jnp.exp(dcol_ref[...]) * s0_ref[...] + kcol_ref[...] * v_ref[...]
    s_ref[...] = s_new
    o = jnp.sum(qcol_ref[...] * s_new, axis=2)
    r = r_ref[...].astype(F32)
    o_ref[...] = (_rms(o, gn_ref[...]) * (r * _sigmoid(r))).astype(BF16)


def _gla_step(s0, gq, gk, gg, gv, gr, g_gla_norm):
    db = s0.shape[0]
    tb = math.gcd(db, 8)
    col = lambda a: jnp.broadcast_to(a.reshape(db, GLA_HEADS, GLA_DK, 1), (db, GLA_HEADS, GLA_DK, GLA_DV))
    big = pl.BlockSpec((tb, GLA_HEADS, GLA_DK, GLA_DV), lambda i: (i, 0, 0, 0))
    return pl.pallas_call(
        _gla_step_kernel,
        grid=(db // tb,),
        in_specs=[big, big, big, big,
                  pl.BlockSpec((tb, GLA_HEADS, 1, GLA_DV), lambda i: (i, 0, 0, 0)),
                  pl.BlockSpec((tb, GLA_HEADS, GLA_DV), lambda i: (i, 0, 0)),
                  pl.BlockSpec((1, 1, GLA_DV), lambda i: (0, 0, 0))],
        out_specs=[big, pl.BlockSpec((tb, GLA_HEADS, GLA_DV), lambda i: (i, 0, 0))],
        out_shape=[jax.ShapeDtypeStruct(s0.shape, F32), jax.ShapeDtypeStruct((db, GLA_HEADS, GLA_DV), BF16)],
        compiler_params=_params("parallel"),
    )(s0, col(gg), col(gk), col(gq), gv.reshape(db, GLA_HEADS, 1, GLA_DV), gr.reshape(db, GLA_HEADS, GLA_DV),
      g_gla_norm.reshape(1, 1, GLA_DV))


def _paged_kernel(pt_ref, q_ref, kn_ref, vn_ref, lq1_ref, lk1_ref, lq2_ref, lk2_ref, gdn_ref, kc_ref, vc_ref,
                  o_ref, kbuf, vbuf, sem, qm_scr, m_scr, l_scr, acc_scr, *, pages):
    seq = pl.program_id(0)
    step = pl.program_id(1)
    n_steps = pl.num_programs(1)
    lin = seq * n_steps + step
    slot = lin % 2
    width = DIFF_HEADS * 2 * DIFF_HD
    n_maps = 2 * DIFF_HEADS

    def page_copies(lin_idx, slot_idx):
        b = lin_idx // n_steps
        first = (lin_idx % n_steps) * pages
        out = []
        for i in range(pages):
            page = pt_ref[b, first + i]
            out.append(pltpu.make_async_copy(kc_ref.at[page], kbuf.at[slot_idx, i], sem.at[0, slot_idx]))
            out.append(pltpu.make_async_copy(vc_ref.at[page], vbuf.at[slot_idx, i], sem.at[1, slot_idx]))
        return out

    @pl.when(lin == 0)
    def _():
        for cp in page_copies(lin, slot):
            cp.start()

    @pl.when(lin + 1 < pl.num_programs(0) * n_steps)
    def _():
        for cp in page_copies(lin + 1, 1 - slot):
            cp.start()

    for cp in page_copies(lin, slot):
        cp.wait()
    k_refs = [kbuf.at[slot, i] for i in range(pages)]
    v_refs = [vbuf.at[slot, i] for i in range(pages)]

    @pl.when(step == 0)
    def _():
        row = lax.broadcasted_iota(jnp.int32, (n_maps, width), 0)
        lane = lax.broadcasted_iota(jnp.int32, (n_maps, width), 1)
        q_rows = jnp.where(lane // DIFF_HD == row, jnp.broadcast_to(q_ref[0], (n_maps, width)), 0.0)
        qm_scr[...] = q_rows
        m_scr[...] = jnp.sum(q_rows * kn_ref[0], axis=-1, keepdims=True)
        l_scr[...] = jnp.ones_like(l_scr)
        acc_scr[...] = jnp.broadcast_to(vn_ref[0], acc_scr.shape)

    qm = qm_scr[...].astype(BF16)
    s = jnp.concatenate([_dot(qm, k_refs[i][...].astype(BF16)) for i in range(pages)], axis=1)
    m_prev = m_scr[...]
    m_new = jnp.maximum(m_prev, jnp.max(s, axis=-1, keepdims=True))
    alpha = jnp.exp2(m_prev - m_new)
    p = jnp.exp2(s - m_new)
    l_scr[...] = alpha * l_scr[...] + jnp.sum(p, axis=-1, keepdims=True)
    acc = alpha * acc_scr[...]
    for i in range(pages):
        p_i = p[:, i * PAGE_SIZE:(i + 1) * PAGE_SIZE].astype(BF16)
        acc = acc + jnp.concatenate(
            [_dot(p_i, v_refs[i][pl.ds(h, PAGE_SIZE, stride=DIFF_HEADS), :].astype(BF16))
             for h in range(DIFF_HEADS)], axis=1)
    acc_scr[...] = acc
    m_scr[...] = m_new

    @pl.when(step == pl.num_programs(1) - 1)
    def _():
        o = acc_scr[...] / l_scr[...]
        for h in range(DIFF_HEADS):
            sl = slice(h * DIFF_VD, (h + 1) * DIFF_VD)
            od = _diff_combine(o[2 * h:2 * h + 1, sl], o[2 * h + 1:2 * h + 2, sl],
                               lq1_ref, lk1_ref, lq2_ref, lk2_ref, gdn_ref)
            o_ref[0, :, sl] = od.astype(BF16)


def _diff_attn_paged(dq, dk, dv, cache_kt, cache_v, page_table, lam_params, g_diff_norm):
    db, n_pages = page_table.shape
    pages = math.gcd(n_pages, 16)
    width = DIFF_HEADS * 2 * DIFF_HD
    row = pl.BlockSpec((1, 1, width), lambda b, s, pt: (b, 0, 0))
    vec = lambda w: pl.BlockSpec((1, w), lambda b, s, pt: (0, 0))

    n_maps = 2 * DIFF_HEADS
    hbm = pl.BlockSpec(memory_space=pl.ANY)
    return pl.pallas_call(
        functools.partial(_paged_kernel, pages=pages),
        grid_spec=pltpu.PrefetchScalarGridSpec(
            num_scalar_prefetch=1,
            grid=(db, n_pages // pages),
            in_specs=[row, row, row, vec(DIFF_HD), vec(DIFF_HD), vec(DIFF_HD), vec(DIFF_HD), vec(DIFF_VD),
                      hbm, hbm],
            out_specs=row,
            scratch_shapes=[pltpu.VMEM((2, pages, width, PAGE_SIZE), F32),
                            pltpu.VMEM((2, pages, PAGE_SIZE * DIFF_HEADS, DIFF_VD), F32),
                            pltpu.SemaphoreType.DMA((2, 2)),
                            pltpu.VMEM((n_maps, width), F32), pltpu.VMEM((n_maps, 1), F32),
                            pltpu.VMEM((n_maps, 1), F32), pltpu.VMEM((n_maps, width), F32)]),
        out_shape=jax.ShapeDtypeStruct((db, 1, width), BF16),
        compiler_params=_params("arbitrary", "arbitrary"),
    )(page_table, dq.reshape(db, 1, width), dk.reshape(db, 1, width), dv.reshape(db, 1, width), *lam_params,
      g_diff_norm, cache_kt, cache_v)


def _cross_sample_kernel(q_ref, mk_ref, mv_ref, o_ref, *, n_mem, seqs):
    rows = n_mem * X_HEADS
    row = lax.broadcasted_iota(jnp.int32, (8, rows), 0)
    lane = lax.broadcasted_iota(jnp.int32, (8, rows), 1)
    own_head = lane % X_HEADS == row % X_HEADS
    for t in range(seqs):
        q = q_ref[t].astype(F32)
        q_rows = jnp.concatenate([q[:, h * X_HD:(h + 1) * X_HD] for h in range(X_HEADS)]
                                 + [jnp.zeros((8 - X_HEADS, X_HD), F32)], axis=0).astype(BF16)
        s = _dot_nt(q_rows, mk_ref[t].astype(BF16)) * (X_HD ** -0.5)
        s = jnp.where(own_head, s, -jnp.inf)
        p = jnp.exp(s - jnp.max(s, axis=-1, keepdims=True))
        pr = p / jnp.sum(p, axis=-1, keepdims=True)
        o = _dot(pr.astype(BF16), mv_ref[t].astype(BF16))
        for h in range(X_HEADS):
            o_ref[t, :, h * X_HD:(h + 1) * X_HD] = o[h:h + 1, :].astype(BF16)


def _cross_sample(xq, mem_k, mem_v):
    db, rows, _ = mem_k.shape
    width = X_HEADS * X_HD
    seqs = math.gcd(db, 4)
    row = pl.BlockSpec((seqs, 1, width), lambda b: (b, 0, 0))
    mem = pl.BlockSpec((seqs, rows, X_HD), lambda b: (b, 0, 0))
    return pl.pallas_call(
        functools.partial(_cross_sample_kernel, n_mem=rows // X_HEADS, seqs=seqs),
        grid=(db // seqs,),
        in_specs=[row, mem, mem],
        out_specs=row,
        out_shape=jax.ShapeDtypeStruct((db, 1, width), BF16),
        compiler_params=_params("parallel"),
    )(xq.reshape(db, 1, width), mem_k, mem_v)


def _rope_tables(pos):
    half = DIFF_HD // 2
    inv = jnp.exp(-math.log(ROPE_THETA) * jnp.arange(half, dtype=F32) * (2.0 / DIFF_HD))
    ang = pos[:, None] * inv[None, :]
    cos, sin = jnp.cos(ang), jnp.sin(ang)
    cos_l = jnp.concatenate([cos, cos, cos, cos], axis=1)
    sin_l = jnp.concatenate([-sin, sin, -sin, sin], axis=1)
    return cos_l, sin_l, cos.T, sin.T


def _forward(x_prompt, x_sample, cache_k_diff, cache_v_diff, state_gla, cache_mem_k, cache_mem_v, page_table,
             mem_prompt, g_norm1, w_in, w_gla_a2, b_gla_a, g_gla_norm, lambda_q1, lambda_k1, lambda_q2,
             lambda_k2, g_diff_norm, w_br_a, w_br_b, w_gate, b_gate, w_o, g_norm2, g_mem, w_xq, w_xk, w_xv,
             w_xo, g_norm3, w_router, b_router, w_gate_up, b_gate_up, w_down, b_down, g_final):
    b_p, t_p, d = x_prompt.shape
    db, t_s, _ = x_sample.shape
    assert t_s == 1 and d == D_MODEL and w_in.shape[0] == 1
    n_pages = page_table.shape[1]
    n_pool = cache_k_diff.shape[1]
    n_mem = mem_prompt.shape[1]
    row = lambda a: a.reshape(1, -1).astype(F32)

    wi = w_in[0]
    w_main = jnp.concatenate([wi[:, :1536], wi[:, 1552:2064], wi[:, 2576:]], axis=1).astype(BF16)
    w_kt = wi[:, 2064:2576].T.astype(BF16)
    w_ga = jnp.pad(wi[:, 1536:1552], ((0, 0), (0, LANES - GLA_GATE_RANK))).astype(BF16)
    w_a2 = jnp.pad(w_gla_a2[0], ((0, LANES - GLA_GATE_RANK), (0, 0))).astype(BF16)
    g1, g2, g3, gm, gf = row(g_norm1[0]), row(g_norm2[0]), row(g_norm3[0]), row(g_mem[0]), row(g_final)
    lam_params = [row(lambda_q1[0]), row(lambda_k1[0]), row(lambda_q2[0]), row(lambda_k2[0])]
    gdn, ggn = row(g_diff_norm[0]), row(g_gla_norm[0])
    merge_w = [g1, w_gate[0].astype(BF16), row(b_gate[0]), w_br_a[0].astype(BF16), w_br_b[0].astype(BF16),
               w_o[0].astype(BF16), g2, w_xq[0].astype(BF16)]
    wr = jnp.pad(w_router[0], ((0, 0), (0, LANES - N_EXPERTS)))
    wr_hi = wr.astype(BF16)
    wr_lo = (wr - wr_hi.astype(F32)).astype(BF16)
    post_w = [w_xo[0].astype(BF16), g3, wr_hi, wr_lo, row(jnp.pad(b_router[0], (0, LANES - N_EXPERTS)))]
    bgu = b_gate_up[0].reshape(N_EXPERTS, D_FF, 2).transpose(0, 2, 1).reshape(N_EXPERTS, 1, 2 * D_FF)
    moe_w = [w_gate_up[0], bgu.astype(F32), w_down[0], b_down[0].reshape(N_EXPERTS, 1, d)]

    m_p = b_p * t_p
    tm_p = math.gcd(t_p, 512)
    xp = x_prompt.reshape(m_p, d)
    mkf, mvf, mkb, mvb = _memkv(mem_prompt.reshape(b_p * n_mem, d), gm, w_xk[0].astype(BF16),
                                w_xv[0].astype(BF16))
    width = DIFF_HEADS * 2 * DIFF_HD
    gq, gk, gg, gv, gr, dq, dvf, dvb, dkt_f, dkt_b = _proj(
        xp, g1, w_main, w_kt, w_ga, w_a2, row(b_gla_a[0]), _rope_tables(jnp.arange(t_p, dtype=F32)), tm_p, t_p)
    og, gla_p = _gla_prompt(gq, gk, gg, gv, gr, ggn, b_p, t_p)
    od = _diff_attn_prompt(dq, dkt_b, dvb, lam_params, gdn, b_p, t_p)
    k_prompt = dkt_f.reshape(b_p, DIFF_HEADS, 2, DIFF_HD, t_p).transpose(0, 4, 1, 2, 3)
    x1, xq = _merge(xp, og, od, *merge_w, tm_p)
    x2, h3, ti, tg = _post(x1, xq, *post_w, tm_p, mem=(mkb, mvb), seq=t_p)

    tm_s = db
    xs = x_sample.reshape(db, d)
    past = n_pages * PAGE_SIZE
    sq, sk, sg, sv, sr, sdq, sdvf, _, sdkt_f, _ = _proj(
        xs, g1, w_main, w_kt, w_ga, w_a2, row(b_gla_a[0]), _rope_tables(jnp.full((db,), past, dtype=F32)), tm_s, db)
    sdkf = sdkt_f[0].T
    gla_s, og_s = _gla_step(state_gla[0], sq, sk, sg, sv, sr, ggn)
    cache_kt = jnp.transpose(cache_k_diff[0], (0, 2, 3, 4, 1)).reshape(n_pool, width, PAGE_SIZE)
    cache_v2 = cache_v_diff[0].reshape(n_pool, PAGE_SIZE * DIFF_HEADS, DIFF_VD)
    od_s = _diff_attn_paged(sdq.astype(F32), sdkf, sdvf, cache_kt, cache_v2, page_table, lam_params, gdn)
    x1s, xqs = _merge(xs, og_s.reshape(db, 512), od_s.reshape(db, 512), *merge_w, tm_s)
    o_s = _cross_sample(xqs, cache_mem_k[0].reshape(db, n_mem * X_HEADS, X_HD),
                        cache_mem_v[0].reshape(db, n_mem * X_HEADS, X_HD))
    x2s, h3s, tis, tgs = _post(x1s, o_s.reshape(db, 512), *post_w, tm_s)
    y_p, y_s = _moe([(x2, h3, ti, tg, math.gcd(m_p, 256)), (x2s, h3s, tis, tgs, db)], moe_w, gf, 256)

    return (y_p.reshape(b_p, t_p, d), y_s.reshape(db, 1, d),
            k_prompt[None], dvf.reshape(1, b_p, t_p, DIFF_HEADS, DIFF_VD),
            gla_p[None], mkf.reshape(1, b_p, n_mem, X_HEADS, X_HD), mvf.reshape(1, b_p, n_mem, X_HEADS, X_HD),
            sdkf.reshape(1, db, 1, DIFF_HEADS, 2, DIFF_HD), sdvf.reshape(1, db, 1, DIFF_HEADS, DIFF_VD),
            gla_s[None])


def kernel(x_prompt, x_sample, cache_k_diff, cache_v_diff, state_gla, cache_mem_k, cache_mem_v, page_table, mem_prompt, g_norm1, w_in, w_gla_a2, b_gla_a, g_gla_norm, lambda_q1, lambda_k1, lambda_q2, lambda_k2, g_diff_norm, w_br_a, w_br_b, w_gate, b_gate, w_o, g_norm2, g_mem, w_xq, w_xk, w_xv, w_xo, g_norm3, w_router, b_router, w_gate_up, b_gate_up, w_down, b_down, g_final):
    return _forward(x_prompt, x_sample, cache_k_diff, cache_v_diff, state_gla, cache_mem_k, cache_mem_v,
                    page_table, mem_prompt, g_norm1, w_in, w_gla_a2, b_gla_a, g_gla_norm, lambda_q1, lambda_k1,
                    lambda_q2, lambda_k2, g_diff_norm, w_br_a, w_br_b, w_gate, b_gate, w_o, g_norm2, g_mem,
                    w_xq, w_xk, w_xv, w_xo, g_norm3, w_router, b_router, w_gate_up, b_gate_up, w_down, b_down,
                    g_final)
```

```python
import functools
import math

import jax
import jax.numpy as jnp
from jax import lax
from jax.experimental import pallas as pl
from jax.experimental.pallas import tpu as pltpu

F32 = jnp.float32
BF16 = jnp.bfloat16

D_MODEL = 1024
GLA_HEADS = 4
GLA_DK = 64
GLA_DV = 128
GLA_GATE_RANK = 16
GLA_GATE_NORM = 16.0
DIFF_HEADS = 4
DIFF_HD = 64
DIFF_VD = 128
X_HEADS = 4
X_HD = 128
N_EXPERTS = 32
TOP_K = 4
D_FF = 1024
SWIGLU_LIMIT = 7.0
SWIGLU_ALPHA = 1.702
PAGE_SIZE = 128
ROPE_THETA = 10000.0
EPS = 1e-6
LAMBDA_INIT = 0.8 - 0.6 * math.exp(-0.3 * 0)

LANES = 128
ROW_TILE = 8
LOG2E = math.log2(math.e)
ATTN_BLOCK = 512
PAGES_PER_UNIT = 16
GLA_CHUNK = 64
GLA_SUB = 16
GLA_EXP_CLAMP = 80.0
VMEM_LIMIT = 56 * 1024 * 1024


def _dot(a, b):
    return jnp.dot(a, b, preferred_element_type=F32)


def _dot_nt(a, b):
    return lax.dot_general(a, b, (((1,), (1,)), ((), ())), preferred_element_type=F32)


def _dot_tn(a, b):
    return lax.dot_general(a, b, (((0,), (0,)), ((), ())), preferred_element_type=F32)


def _rms(x, g):
    return x * lax.rsqrt(jnp.mean(x * x, axis=-1, keepdims=True) + EPS) * g


def _sigmoid(x):
    return 1.0 / (1.0 + jnp.exp(-x))


def _params(*sem):
    return pltpu.CompilerParams(dimension_semantics=sem, vmem_limit_bytes=VMEM_LIMIT)


def _const_spec(shape):
    nd = len(shape)
    return pl.BlockSpec(shape, lambda *_: (0,) * nd)


def _proj_kernel(x_ref, g1_ref, w_ref, wkt_ref, wga_ref, wa2_ref, ba_ref, cos_ref, sin_ref, cost_ref, sint_ref,
                 gq_ref, gk_ref, gg_ref, gv_ref, gr_ref, dq_ref, dvf_ref, dvb_ref, dkf_ref, dkb_ref):
    x = x_ref[...]
    u = _rms(x, g1_ref[...]).astype(BF16)

    def seg(lo, hi):
        return _dot(u, w_ref[:, lo:hi])

    gq_ref[...] = seg(0, 256) * (GLA_DK ** -0.5)
    gk_ref[...] = seg(256, 512)
    gv_ref[...] = seg(512, 1024).astype(BF16)
    gr_ref[...] = seg(1024, 1536).astype(BF16)
    ga = _dot(u, wga_ref[...])
    z = _dot(ga.astype(BF16), wa2_ref[...]) + ba_ref[...]
    log_sig = jnp.minimum(z, 0.0) - jnp.log(1.0 + jnp.exp(-jnp.abs(z)))
    gg_ref[...] = log_sig * (1.0 / GLA_GATE_NORM)

    cos = jnp.concatenate([cos_ref[...]] * 4, axis=1)
    sin = jnp.concatenate([sin_ref[...]] * 4, axis=1)
    width = 4 * LANES
    lane = lax.broadcasted_iota(jnp.int32, (1, width), 1)
    first_half = (lane % DIFF_HD) < (DIFF_HD // 2)

    def rope(v):
        partner = jnp.where(first_half, pltpu.roll(v, width - DIFF_HD // 2, 1), pltpu.roll(v, DIFF_HD // 2, 1))
        return v * cos + partner * sin

    dq = rope(seg(1536, 2048))
    dq_ref[...] = (dq * (DIFF_HD ** -0.5 * LOG2E)).astype(BF16)
    dv = seg(2048, 2560)
    dvf_ref[...] = dv
    dvb_ref[...] = dv.astype(BF16)
    k_t = _dot_nt(wkt_ref[...], u)
    cos_t = cost_ref[...]
    sin_t = sint_ref[...]
    half = DIFF_HD // 2
    slabs = []
    for grp in range(2 * DIFF_HEADS):
        x1 = k_t[grp * DIFF_HD:grp * DIFF_HD + half, :]
        x2 = k_t[grp * DIFF_HD + half:(grp + 1) * DIFF_HD, :]
        slabs += [x1 * cos_t - x2 * sin_t, x2 * cos_t + x1 * sin_t]
    k_rot = jnp.concatenate(slabs, axis=0)
    dkf_ref[0] = k_rot
    dkb_ref[0] = k_rot.astype(BF16)


def _proj(x, g1, w_main, w_kt, w_ga, w_a2, b_a, rope_tables, tm, seq):
    m = x.shape[0]
    cos, sin, cos_t, sin_t = rope_tables
    n_seq = seq // tm
    width = DIFF_HEADS * 2 * DIFF_HD
    tok = lambda w: pl.BlockSpec((tm, w), lambda i: (i, 0))
    rope = pl.BlockSpec((tm, LANES), lambda i: (i % n_seq, 0))
    rope_t = pl.BlockSpec((DIFF_HD // 2, tm), lambda i: (0, i % n_seq))
    key_t = pl.BlockSpec((1, width, tm), lambda i: (i // n_seq, 0, i % n_seq))
    outs = [(256, F32), (256, F32), (256, F32), (512, BF16), (512, BF16), (512, BF16), (512, F32), (512, BF16)]
    return pl.pallas_call(
        _proj_kernel,
        grid=(m // tm,),
        in_specs=[tok(D_MODEL), _const_spec((1, D_MODEL)), _const_spec(w_main.shape), _const_spec(w_kt.shape),
                  _const_spec(w_ga.shape), _const_spec(w_a2.shape), _const_spec((1, 256)),
                  rope, rope, rope_t, rope_t],
        out_specs=[tok(w) for w, _ in outs] + [key_t, key_t],
        out_shape=[jax.ShapeDtypeStruct((m, w), dt) for w, dt in outs]
        + [jax.ShapeDtypeStruct((m // seq, width, seq), F32), jax.ShapeDtypeStruct((m // seq, width, seq), BF16)],
        compiler_params=_params("parallel"),
    )(x, g1, w_main, w_kt, w_ga, w_a2, b_a, cos, sin, cos_t, sin_t)


def _gla_kernel(q_ref, k_ref, g_ref, v_ref, r_ref, gn_ref, o_ref, s_ref, st_scr, *, chunk, sub, seqs):
    c = pl.program_id(1)

    @pl.when(c == 0)
    def _():
        st_scr[...] = jnp.zeros_like(st_scr)

    for sq in range(seqs):
        _gla_chunk(sq, q_ref, k_ref, g_ref, v_ref, r_ref, gn_ref, o_ref, st_scr, chunk, sub)

    @pl.when(c == pl.num_programs(1) - 1)
    def _():
        for sq in range(seqs):
            for pair in range(GLA_HEADS // 2):
                s_ref[sq, 2 * pair:2 * pair + 2] = st_scr[sq, pair].T.reshape(2, GLA_DK, GLA_DV)


def _gla_chunk(sq, q_ref, k_ref, g_ref, v_ref, r_ref, gn_ref, o_ref, st_scr, chunk, sub):
    g = g_ref[sq]
    row = lax.broadcasted_iota(jnp.int32, (chunk, chunk), 0)
    col = lax.broadcasted_iota(jnp.int32, (chunk, chunk), 1)
    causal = col <= row
    tril = jnp.where(causal, 1.0, 0.0).astype(BF16)
    g_hi = g.astype(BF16)
    r1 = g - g_hi.astype(F32)
    g_mid = r1.astype(BF16)
    g_lo = (r1 - g_mid.astype(F32)).astype(BF16)
    cum = _dot(tril, g_hi) + _dot(tril, g_mid) + _dot(tril, g_lo)
    q = q_ref[sq]
    k = k_ref[sq]
    last = cum[chunk - 1:chunk, :]
    q_dec = q * jnp.exp(cum)
    k_dec = k * jnp.exp(last - cum)
    lane = lax.broadcasted_iota(jnp.int32, (1, LANES), 1)
    head_mask = (lane < GLA_DK, lane >= GLA_DK)
    gn = gn_ref[...]

    for pair in range(GLA_HEADS // 2):
        sl = slice(pair * LANES, (pair + 1) * LANES)
        state_t = st_scr[sq, pair]
        cum_p, q_p, k_p = cum[:, sl], q[:, sl], k[:, sl]
        q_dec_p, k_dec_p = q_dec[:, sl], k_dec[:, sl]
        k_anchor = []
        for i in range(chunk // sub):
            anchor = cum_p[i * sub:i * sub + 1, :]
            k_anchor.append((k_p * jnp.exp(jnp.minimum(anchor - cum_p, GLA_EXP_CLAMP))).astype(BF16))
        state_new = jnp.exp(last[:, sl]) * state_t
        state_bf = state_t.astype(BF16)
        for hh in range(2):
            h = 2 * pair + hh
            hm = head_mask[hh]
            v_h = v_ref[sq, :, h * GLA_DV:(h + 1) * GLA_DV]
            rows = []
            for i in range(chunk // sub):
                r0 = i * sub
                q_a = q_p[r0:r0 + sub, :] * jnp.exp(cum_p[r0:r0 + sub, :] - cum_p[r0:r0 + 1, :])
                rows.append(_dot_nt(jnp.where(hm, q_a, 0.0).astype(BF16), k_anchor[i]))
            a = jnp.where(causal, jnp.concatenate(rows, axis=0), 0.0).astype(BF16)
            o = _dot_nt(jnp.where(hm, q_dec_p, 0.0).astype(BF16), state_bf) + _dot(a, v_h)
            state_new = state_new + _dot_tn(v_h, jnp.where(hm, k_dec_p, 0.0).astype(BF16))
            r = r_ref[sq, :, h * GLA_DV:(h + 1) * GLA_DV].astype(F32)
            o_ref[sq, :, h * GLA_DV:(h + 1) * GLA_DV] = (_rms(o, gn) * (r * _sigmoid(r))).astype(BF16)
        st_scr[sq, pair] = state_new


def _gla_prompt(gq, gk, gg, gv, gr, g_gla_norm, batch, seq):
    chunk = math.gcd(seq, GLA_CHUNK)
    sub = math.gcd(chunk, GLA_SUB)
    seqs = math.gcd(batch, 4)
    tok = lambda w: pl.BlockSpec((seqs, chunk, w), lambda b, c: (b, c, 0))
    per_seq = lambda a: a.reshape(batch, seq, a.shape[-1])
    og, state = pl.pallas_call(
        functools.partial(_gla_kernel, chunk=chunk, sub=sub, seqs=seqs),
        grid=(batch // seqs, seq // chunk),
        in_specs=[tok(256), tok(256), tok(256), tok(512), tok(512), _const_spec((1, GLA_DV))],
        out_specs=[tok(512), pl.BlockSpec((seqs, GLA_HEADS, GLA_DK, GLA_DV), lambda b, c: (b, 0, 0, 0))],
        out_shape=[jax.ShapeDtypeStruct((batch, seq, 512), BF16),
                   jax.ShapeDtypeStruct((batch, GLA_HEADS, GLA_DK, GLA_DV), F32)],
        scratch_shapes=[pltpu.VMEM((seqs, GLA_HEADS // 2, GLA_DV, 2 * GLA_DK), F32)],
        compiler_params=_params("parallel", "arbitrary"),
    )(per_seq(gq), per_seq(gk), per_seq(gg), per_seq(gv), per_seq(gr), g_gla_norm)
    return og.reshape(batch * seq, 512), state


def _diff_combine(o1, o2, lq1_ref, lk1_ref, lq2_ref, lk2_ref, gdn_ref):
    lam = (jnp.exp(jnp.sum(lq1_ref[...] * lk1_ref[...], axis=-1, keepdims=True))
           - jnp.exp(jnp.sum(lq2_ref[...] * lk2_ref[...], axis=-1, keepdims=True)) + LAMBDA_INIT)
    od = o1 - lam * o2
    return _rms(od, gdn_ref[...]) * (1.0 - LAMBDA_INIT)


def _attn_fused_kernel(qi_tab, ki_tab, pt_ref, q_ref, k_ref, v_ref, lq1_ref, lk1_ref, lq2_ref, lk2_ref, gdn_ref,
                       sq_ref, skn_ref, svn_ref, kc_ref, vc_ref, o_ref, os_ref,
                       m_scr, acc_scr, kbuf, vbuf, sem, qm_scr, pm_scr, pl_scr, pacc_scr,
                       *, blk, pages, n_units, units_per_seq, units_per_step):
    step = pl.program_id(2)
    lin = (pl.program_id(0) * pl.num_programs(1) + pl.program_id(1)) * pl.num_programs(2) + step
    lam_refs = (lq1_ref, lk1_ref, lq2_ref, lk2_ref)
    for u in range(units_per_step):
        unit = lin * units_per_step + u

        @pl.when(unit < n_units)
        def _(unit=unit):
            _paged_unit(unit, n_units, units_per_seq, pt_ref, sq_ref, skn_ref, svn_ref, lam_refs, gdn_ref,
                        kc_ref, vc_ref, os_ref, kbuf, vbuf, sem, qm_scr, pm_scr, pl_scr, pacc_scr, pages)

    _attn_tile(qi_tab[step], ki_tab[step], q_ref, k_ref, v_ref, lam_refs, gdn_ref, o_ref, m_scr, acc_scr, blk)


def _attn_tile(qi, ki, q_ref, k_ref, v_ref, lam_refs, gdn_ref, o_ref, m_scr, acc_scr, blk):
    lq1_ref, lk1_ref, lq2_ref, lk2_ref = lam_refs

    @pl.when(ki == 0)
    def _():
        m_scr[...] = jnp.full_like(m_scr, -jnp.inf)
        acc_scr[...] = jnp.zeros_like(acc_scr)

    lane = lax.broadcasted_iota(jnp.int32, (1, LANES), 1)
    map_mask = (lane < DIFF_HD, lane >= DIFF_HD)

    def update(diagonal):
        q = q_ref[...]
        k_t = k_ref[0]
        v = v_ref[...]
        if diagonal:
            row = lax.broadcasted_iota(jnp.int32, (blk, blk), 0)
            col = lax.broadcasted_iota(jnp.int32, (blk, blk), 1)
            keep = col <= row
        v_ext = jnp.concatenate([v, jnp.ones_like(v)], axis=1)
        scores = [_dot(jnp.where(map_mask[j], q, jnp.zeros_like(q)), k_t) for j in range(2)]
        for j in range(2):
            s = scores[j]
            if diagonal:
                s = jnp.where(keep, s, -jnp.inf)
            m_prev = m_scr[j]
            m_new = jnp.maximum(m_prev, jnp.max(s, axis=-1, keepdims=True))
            alpha = jnp.exp2(m_prev - m_new)
            p = jnp.exp2(s - jnp.concatenate([m_new] * (blk // LANES), axis=1)).astype(BF16)
            acc_scr[j] = jnp.concatenate([alpha, alpha], axis=1) * acc_scr[j] + _dot(p, v_ext)
            m_scr[j] = m_new

    @pl.when(ki < qi)
    def _():
        update(False)

    @pl.when(ki == qi)
    def _():
        update(True)
        o1 = acc_scr[0, :, :DIFF_VD] / acc_scr[0, :, DIFF_VD:]
        o2 = acc_scr[1, :, :DIFF_VD] / acc_scr[1, :, DIFF_VD:]
        o_ref[...] = _diff_combine(o1, o2, lq1_ref, lk1_ref, lq2_ref, lk2_ref, gdn_ref).astype(BF16)


def _diff_attention(dq, dkt, dv, sdq, sdk, sdv, cache_kt, cache_v, page_table, lam_params, g_diff_norm, batch, seq):
    blk = math.gcd(seq, ATTN_BLOCK)
    nb = seq // blk
    pairs = [(qi, ki) for qi in range(nb) for ki in range(qi + 1)]
    qi_tab = jnp.array([p[0] for p in pairs], jnp.int32)
    ki_tab = jnp.array([p[1] for p in pairs], jnp.int32)
    db, n_pages = page_table.shape
    pages = math.gcd(n_pages, PAGES_PER_UNIT)
    units_per_seq = n_pages // pages
    n_units = db * units_per_seq
    n_steps = batch * DIFF_HEADS * len(pairs)
    units_per_step = -(-n_units // n_steps)
    width = DIFF_HEADS * 2 * DIFF_HD
    n_maps = 2 * DIFF_HEADS
    q_spec = pl.BlockSpec((blk, LANES), lambda b, h, s, qt, kt, pt: (b * nb + qt[s], h))
    kv_spec = pl.BlockSpec((blk, LANES), lambda b, h, s, qt, kt, pt: (b * nb + kt[s], h))
    kt_spec = pl.BlockSpec((1, LANES, blk), lambda b, h, s, qt, kt, pt: (b, h, kt[s]))
    vec = lambda w: pl.BlockSpec((1, w), lambda b, h, s, qt, kt, pt: (0, 0))
    rows = pl.BlockSpec((db, 1, width), lambda b, h, s, qt, kt, pt: (0, 0, 0))
    hbm = pl.BlockSpec(memory_space=pl.ANY)
    return pl.pallas_call(
        functools.partial(_attn_fused_kernel, blk=blk, pages=pages, n_units=n_units, units_per_seq=units_per_seq,
                          units_per_step=units_per_step),
        grid_spec=pltpu.PrefetchScalarGridSpec(
            num_scalar_prefetch=3,
            grid=(batch, DIFF_HEADS, len(pairs)),
            in_specs=[q_spec, kt_spec, kv_spec, vec(DIFF_HD), vec(DIFF_HD), vec(DIFF_HD), vec(DIFF_HD),
                      vec(DIFF_VD), rows, rows, rows, hbm, hbm],
            out_specs=[q_spec, rows],
            scratch_shapes=[pltpu.VMEM((2, blk, LANES), F32), pltpu.VMEM((2, blk, 2 * DIFF_VD), F32),
                            pltpu.VMEM((2, pages, width, PAGE_SIZE), F32),
                            pltpu.VMEM((2, pages, PAGE_SIZE * DIFF_HEADS, DIFF_VD), F32),
                            pltpu.SemaphoreType.DMA((2, 2)),
                            pltpu.VMEM((n_maps, width), F32), pltpu.VMEM((n_maps, 1), F32),
                            pltpu.VMEM((n_maps, 1), F32), pltpu.VMEM((n_maps, width), F32)]),
        out_shape=[jax.ShapeDtypeStruct((batch * seq, DIFF_HEADS * DIFF_VD), BF16),
                   jax.ShapeDtypeStruct((db, 1, width), BF16)],
        compiler_params=_params("arbitrary", "arbitrary", "arbitrary"),
    )(qi_tab, ki_tab, page_table, dq, dkt, dv, *lam_params, g_diff_norm, sdq.reshape(db, 1, width),
      sdk.reshape(db, 1, width), sdv.reshape(db, 1, width), cache_kt, cache_v)


def _merge_kernel(x_ref, og_ref, od_ref, g1_ref, wgate_ref, bgate_ref, wbra_ref, wbrb_ref, wo_ref, g2_ref,
                  wxq_ref, x1_ref, xq_ref):
    x = x_ref[...]
    u = _rms(x, g1_ref[...]).astype(BF16)
    gates = _sigmoid(_dot(u, wgate_ref[...]) + bgate_ref[...])
    merged = (gates[:, :D_MODEL] * _dot(og_ref[...], wbra_ref[...])
              + gates[:, D_MODEL:] * _dot(od_ref[...], wbrb_ref[...]))
    x1 = x + _dot(merged.astype(BF16), wo_ref[...])
    x1_ref[...] = x1
    xq_ref[...] = _dot(_rms(x1, g2_ref[...]).astype(BF16), wxq_ref[...]).astype(BF16)


def _merge(x, og, od, g1, w_gate, b_gate, w_br_a, w_br_b, w_o, g2, w_xq, tm):
    m = x.shape[0]
    tok = lambda w: pl.BlockSpec((tm, w), lambda i: (i, 0))
    consts = [g1, w_gate, b_gate, w_br_a, w_br_b, w_o, g2, w_xq]
    return pl.pallas_call(
        _merge_kernel,
        grid=(m // tm,),
        in_specs=[tok(D_MODEL), tok(512), tok(512)] + [_const_spec(c.shape) for c in consts],
        out_specs=[tok(D_MODEL), tok(512)],
        out_shape=[jax.ShapeDtypeStruct((m, D_MODEL), F32), jax.ShapeDtypeStruct((m, 512), BF16)],
        compiler_params=_params("parallel"),
    )(x, og, od, *consts)


def _post_kernel(*refs, shared_memory):
    if shared_memory:
        (x1_ref, xq_ref, mk_ref, mv_ref, wxo_ref, g3_ref, wrh_ref, wrl_ref, br_ref,
         x2_ref, h3_ref, ti_ref, tg_ref) = refs
        xq = xq_ref[...]
        heads = []
        for h in range(X_HEADS):
            sl = slice(h * X_HD, (h + 1) * X_HD)
            s = _dot_nt(xq[:, sl], mk_ref[:, sl]) * (X_HD ** -0.5)
            p = jnp.exp(s - jnp.max(s, axis=-1, keepdims=True))
            pr = p / jnp.sum(p, axis=-1, keepdims=True)
            heads.append(_dot(pr.astype(BF16), mv_ref[:, sl]).astype(BF16))
        o = jnp.concatenate(heads, axis=1)
    else:
        (x1_ref, o_ref, wxo_ref, g3_ref, wrh_ref, wrl_ref, br_ref, x2_ref, h3_ref, ti_ref, tg_ref) = refs
        o = o_ref[...]
    x2 = x1_ref[...] + _dot(o, wxo_ref[...])
    x2_ref[...] = x2
    h3 = _rms(x2, g3_ref[...])
    _to_row_tiles(h3_ref, h3, x2.shape[0])
    h_hi = h3.astype(BF16)
    h_lo = (h3 - h_hi.astype(F32)).astype(BF16)
    logits = (_dot(h_hi, wrh_ref[...]) + _dot(h_lo, wrh_ref[...]) + _dot(h_hi, wrl_ref[...])) + br_ref[...]
    lane = lax.broadcasted_iota(jnp.int32, logits.shape, 1)
    logits = jnp.where(lane < N_EXPERTS, logits, -jnp.inf)
    vals, idxs = [], []
    for _ in range(TOP_K):
        top = jnp.max(logits, axis=-1, keepdims=True)
        idx = jnp.min(jnp.where(logits == top, lane, LANES), axis=-1, keepdims=True)
        vals.append(top)
        idxs.append(idx)
        logits = jnp.where(lane == idx, -jnp.inf, logits)
    exps = [jnp.exp(v - vals[0]) for v in vals]
    denom = exps[0] + exps[1] + exps[2] + exps[3]
    ti = jnp.zeros(lane.shape, jnp.int32)
    tg = jnp.zeros(lane.shape, F32)
    for r in range(TOP_K):
        ti = jnp.where(lane == r, idxs[r], ti)
        tg = jnp.where(lane == r, exps[r] / denom, tg)
    ti_ref[...] = ti
    tg_ref[...] = tg


def _post(x1, attn_in, w_xo, g3, wr_hi, wr_lo, b_r, tm, mem=None, seq=None):
    m = x1.shape[0]
    tok = lambda w: pl.BlockSpec((tm, w), lambda i: (i, 0))
    consts = [w_xo, g3, wr_hi, wr_lo, b_r]
    if mem is not None:
        mk, mv = mem
        n_mem = mk.shape[0] // (m // seq)
        tiles_per_seq = seq // tm
        mem_spec = pl.BlockSpec((n_mem, 512), lambda i: (i // tiles_per_seq, 0))
        in_specs = [tok(D_MODEL), tok(512), mem_spec, mem_spec]
        args = [x1, attn_in, mk, mv]
    else:
        in_specs = [tok(D_MODEL), tok(512)]
        args = [x1, attn_in]
    return pl.pallas_call(
        functools.partial(_post_kernel, shared_memory=mem is not None),
        grid=(m // tm,),
        in_specs=in_specs + [_const_spec(c.shape) for c in consts],
        out_specs=[tok(D_MODEL), pl.BlockSpec((tm * ROW_TILE, LANES), lambda i: (i, 0)), tok(LANES), tok(LANES)],
        out_shape=[jax.ShapeDtypeStruct((m, D_MODEL), F32), jax.ShapeDtypeStruct((m * ROW_TILE, LANES), F32),
                   jax.ShapeDtypeStruct((m, LANES), jnp.int32), jax.ShapeDtypeStruct((m, LANES), F32)],
        compiler_params=_params("parallel"),
    )(*args, *consts)


def _memkv_kernel(m_ref, g_ref, wk_ref, wv_ref, kf_ref, vf_ref, kb_ref, vb_ref):
    u = _rms(m_ref[...], g_ref[...]).astype(BF16)
    mk = _dot(u, wk_ref[...])
    mv = _dot(u, wv_ref[...])
    kf_ref[...] = mk
    vf_ref[...] = mv
    kb_ref[...] = mk.astype(BF16)
    vb_ref[...] = mv.astype(BF16)


def _memkv(mem, g_mem, w_xk, w_xv):
    m = mem.shape[0]
    tm = math.gcd(m, 256)
    tok = lambda w: pl.BlockSpec((tm, w), lambda i: (i, 0))
    return pl.pallas_call(
        _memkv_kernel,
        grid=(m // tm,),
        in_specs=[tok(D_MODEL), _const_spec((1, D_MODEL)), _const_spec(w_xk.shape), _const_spec(w_xv.shape)],
        out_specs=[tok(512)] * 4,
        out_shape=[jax.ShapeDtypeStruct((m, 512), F32)] * 2 + [jax.ShapeDtypeStruct((m, 512), BF16)] * 2,
        compiler_params=_params("parallel"),
    )(mem, g_mem, w_xk, w_xv)


def _row_copy(src_ref, src_row, dst_ref, dst_row, sem):
    src = src_ref.at[pl.ds(pl.multiple_of(src_row * ROW_TILE, ROW_TILE), ROW_TILE), :]
    dst = dst_ref.at[pl.ds(pl.multiple_of(dst_row * ROW_TILE, ROW_TILE), ROW_TILE), :]
    return pltpu.make_async_copy(src, dst, sem)


def _to_row_tiles(ref, value, rows):
    for c in range(D_MODEL // LANES):
        ref[pl.ds(c, rows, stride=ROW_TILE), :] = value[:, c * LANES:(c + 1) * LANES]


def _from_row_tiles(ref, first_row, rows):
    return jnp.concatenate([ref[pl.ds(first_row * ROW_TILE + c, rows, stride=ROW_TILE), :]
                            for c in range(D_MODEL // LANES)], axis=1)


def _run_row_copies(tokens, copies):
    def issue(n, carry):
        for r, cp in enumerate(copies(n)):
            cp.start(priority=r % 2)
        return carry

    def drain(n, carry):
        for cp in copies(n):
            cp.wait()
        return carry

    unroll = math.gcd(tokens, 8)
    lax.fori_loop(0, tokens, issue, 0, unroll=unroll)
    lax.fori_loop(0, tokens, drain, 0, unroll=unroll)


def _dispatch_kernel(slot_ref, h_ref, xs_in_ref, xs_ref, sem, *, tokens):
    del xs_in_ref

    def copies(n):
        return [_row_copy(h_ref, n, xs_ref, slot_ref[0, 0, n * TOP_K + r], sem) for r in range(TOP_K)]

    _run_row_copies(tokens, copies)


def _dispatch(slot, h_tiles, xs, td):
    m = slot.shape[0]
    return pl.pallas_call(
        functools.partial(_dispatch_kernel, tokens=td),
        grid=(m // td,),
        in_specs=[pl.BlockSpec((1, 1, td * TOP_K), lambda i: (i, 0, 0), memory_space=pltpu.SMEM),
                  pl.BlockSpec((td * ROW_TILE, LANES), lambda i: (i, 0)),
                  pl.BlockSpec(memory_space=pl.ANY)],
        out_specs=pl.BlockSpec(memory_space=pl.ANY),
        out_shape=jax.ShapeDtypeStruct(xs.shape, xs.dtype),
        scratch_shapes=[pltpu.SemaphoreType.DMA(())],
        input_output_aliases={2: 0},
        compiler_params=_params("arbitrary"),
    )(slot.reshape(m // td, 1, td * TOP_K), h_tiles, xs)


def _expert_kernel(blk_e, n_used, next_blk, x_ref, wgu_hbm, bgu_ref, wdn_hbm, bdn_ref, y_ref,
                   wgu_f32, wdn_f32, wgu_scr, wdn_scr, sem, count_ref, *, bm):
    i = pl.program_id(0)
    used = i < n_used[0]
    e = blk_e[i]
    new_expert = jnp.logical_or(i == 0, e != blk_e[jnp.maximum(i - 1, 0)])

    def weight_copies(expert, slot):
        return [pltpu.make_async_copy(wgu_hbm.at[expert], wgu_f32.at[slot], sem.at[0, slot]),
                pltpu.make_async_copy(wdn_hbm.at[expert], wdn_f32.at[slot], sem.at[1, slot])]

    @pl.when(i == 0)
    def _():
        count_ref[0] = 0

    @pl.when(jnp.logical_and(used, new_expert))
    def _():
        slot = count_ref[0] % 2
        count_ref[0] = count_ref[0] + 1

        @pl.when(i == 0)
        def _():
            for cp in weight_copies(e, slot):
                cp.start()

        following = next_blk[e]

        @pl.when(following < n_used[0])
        def _():
            for cp in weight_copies(blk_e[jnp.minimum(following, pl.num_programs(0) - 1)], 1 - slot):
                cp.start()

        for cp in weight_copies(e, slot):
            cp.wait()
        wdn_scr[...] = wdn_f32[slot].astype(BF16)
        group = 2 * LANES
        r = lax.broadcasted_iota(jnp.int32, (group, group), 0)
        c = lax.broadcasted_iota(jnp.int32, (group, group), 1)
        perm = jnp.where(r == 2 * (c % LANES) + c // LANES, 1.0, 0.0).astype(BF16)
        for g in range(2 * D_FF // group):
            w = _dot(wgu_f32[slot, :, g * group:(g + 1) * group].astype(BF16), perm).astype(BF16)
            wgu_scr[:, g * LANES:(g + 1) * LANES] = w[:, :LANES]
            wgu_scr[:, D_FF + g * LANES:D_FF + (g + 1) * LANES] = w[:, LANES:]

    @pl.when(used)
    def _():
        x = _from_row_tiles(x_ref, 0, bm).astype(BF16)
        h = _dot(x, wgu_scr[...]) + bgu_ref[0]
        glu = jnp.minimum(h[:, :D_FF], SWIGLU_LIMIT)
        lin = jnp.clip(h[:, D_FF:], -SWIGLU_LIMIT, SWIGLU_LIMIT)
        act = (lin + 1.0) * glu * _sigmoid(SWIGLU_ALPHA * glu)
        _to_row_tiles(y_ref, _dot(act.astype(BF16), wdn_scr[...]) + bdn_ref[0], bm)

    @pl.when(jnp.logical_not(used))
    def _():
        y_ref[...] = jnp.zeros_like(y_ref)


def _experts(xs, blk_e, n_used, next_blk, wgu, bgu, wdn, bdn, bm):
    n_blk = xs.shape[0] // (bm * ROW_TILE)
    rows = pl.BlockSpec((bm * ROW_TILE, LANES), lambda i, be, nu, nb: (i, 0))
    hbm = pl.BlockSpec(memory_space=pl.ANY)
    return pl.pallas_call(
        functools.partial(_expert_kernel, bm=bm),
        grid_spec=pltpu.PrefetchScalarGridSpec(
            num_scalar_prefetch=3,
            grid=(n_blk,),
            in_specs=[rows, hbm,
                      pl.BlockSpec((1, 1, 2 * D_FF), lambda i, be, nu, nb: (be[i], 0, 0)), hbm,
                      pl.BlockSpec((1, 1, D_MODEL), lambda i, be, nu, nb: (be[i], 0, 0))],
            out_specs=rows,
            scratch_shapes=[pltpu.VMEM((2, D_MODEL, 2 * D_FF), F32), pltpu.VMEM((2, D_FF, D_MODEL), F32),
                            pltpu.VMEM((D_MODEL, 2 * D_FF), BF16), pltpu.VMEM((D_FF, D_MODEL), BF16),
                            pltpu.SemaphoreType.DMA((2, 2)), pltpu.SMEM((1,), jnp.int32)]),
        out_shape=jax.ShapeDtypeStruct(xs.shape, F32),
        compiler_params=_params("arbitrary"),
    )(blk_e, n_used, next_blk, xs, wgu, bgu, wdn, bdn)


def _combine_kernel(slot_ref, x2_ref, tg_ref, gf_ref, ys_ref, o_ref, buf, sem, *, tokens):
    def copies(n):
        return [_row_copy(ys_ref, slot_ref[0, 0, n * TOP_K + r], buf, r * tokens + n, sem) for r in range(TOP_K)]

    _run_row_copies(tokens, copies)
    tg = tg_ref[...]
    acc = x2_ref[...]
    for r in range(TOP_K):
        acc = acc + tg[:, r:r + 1] * _from_row_tiles(buf, r * tokens, tokens)
    o_ref[...] = _rms(acc, gf_ref[...])


def _combine(slot, x2, tg, g_final, ys, tc):
    m = x2.shape[0]
    tok = lambda w: pl.BlockSpec((tc, w), lambda i: (i, 0))
    return pl.pallas_call(
        functools.partial(_combine_kernel, tokens=tc),
        grid=(m // tc,),
        in_specs=[pl.BlockSpec((1, 1, tc * TOP_K), lambda i: (i, 0, 0), memory_space=pltpu.SMEM),
                  tok(D_MODEL), tok(LANES), _const_spec((1, D_MODEL)), pl.BlockSpec(memory_space=pl.ANY)],
        out_specs=tok(D_MODEL),
        out_shape=jax.ShapeDtypeStruct((m, D_MODEL), F32),
        scratch_shapes=[pltpu.VMEM((TOP_K * tc * ROW_TILE, LANES), F32), pltpu.SemaphoreType.DMA(())],
        compiler_params=_params("arbitrary"),
    )(slot.reshape(m // tc, 1, tc * TOP_K), x2, tg, g_final, ys)


def _moe(groups, moe_w, g_final, bm):
    wgu, bgu, wdn, bdn = moe_w
    e = jnp.concatenate([g[2][:, :TOP_K] for g in groups], axis=0)
    m = e.shape[0]
    n_asg = m * TOP_K
    onehot = (e[:, :, None] == jnp.arange(N_EXPERTS, dtype=jnp.int32)).astype(jnp.int32)
    per_tok = jnp.sum(onehot, axis=1)
    before = jnp.cumsum(per_tok, axis=0) - per_tok
    counts = jnp.sum(per_tok, axis=0)
    padded = (counts + bm - 1) // bm * bm
    pad_end = jnp.cumsum(padded)
    pad_start = pad_end - padded
    slot = jnp.sum(onehot * (before + pad_start)[:, None, :], axis=-1).astype(jnp.int32)
    n_blk = -(-n_asg // bm) + N_EXPERTS
    blk_start = jnp.arange(n_blk, dtype=jnp.int32) * bm
    blk_e = jnp.minimum(jnp.sum((blk_start[:, None] >= pad_end[None, :]).astype(jnp.int32), axis=1),
                        N_EXPERTS - 1).astype(jnp.int32)
    n_used = (pad_end[-1:] // bm).astype(jnp.int32)
    xs = jnp.zeros((n_blk * bm * ROW_TILE, LANES), F32)
    slots, start = [], 0
    for x2, h_tiles, _, _, tile in groups:
        slots.append(slot[start:start + x2.shape[0]])
        start += x2.shape[0]
        xs = _dispatch(slots[-1], h_tiles, xs, tile)
    next_blk = (pad_end // bm).astype(jnp.int32)
    ys = _experts(xs, blk_e, n_used, next_blk, wgu, bgu, wdn, bdn, bm)
    return [_combine(s, x2, tg, g_final, ys, tile) for s, (x2, _, _, tg, tile) in zip(slots, groups)]


def _gla_step_kernel(s0_ref, dcol_ref, kcol_ref, qcol_ref, v_ref, r_ref, gn_ref, s_ref, o_ref):
    s_new = jnp.exp(dcol_ref[...]) * s0_ref[...] + kcol_ref[...] * v_ref[...]
    s_ref[...] = s_new
    o = jnp.sum(qcol_ref[...] * s_new, axis=2)
    r = r_ref[...].astype(F32)
    o_ref[...] = (_rms(o, gn_ref[...]) * (r * _sigmoid(r))).astype(BF16)


def _gla_step(s0, gq, gk, gg, gv, gr, g_gla_norm):
    db = s0.shape[0]
    tb = math.gcd(db, 8)
    col = lambda a: jnp.broadcast_to(a.reshape(db, GLA_HEADS, GLA_DK, 1), (db, GLA_HEADS, GLA_DK, GLA_DV))
    big = pl.BlockSpec((tb, GLA_HEADS, GLA_DK, GLA_DV), lambda i: (i, 0, 0, 0))
    return pl.pallas_call(
        _gla_step_kernel,
        grid=(db // tb,),
        in_specs=[big, big, big, big,
                  pl.BlockSpec((tb, GLA_HEADS, 1, GLA_DV), lambda i: (i, 0, 0, 0)),
                  pl.BlockSpec((tb, GLA_HEADS, GLA_DV), lambda i: (i, 0, 0)),
                  pl.BlockSpec((1, 1, GLA_DV), lambda i: (0, 0, 0))],
        out_specs=[big, pl.BlockSpec((tb, GLA_HEADS, GLA_DV), lambda i: (i, 0, 0))],
        out_shape=[jax.ShapeDtypeStruct(s0.shape, F32), jax.ShapeDtypeStruct((db, GLA_HEADS, GLA_DV), BF16)],
        compiler_params=_params("parallel"),
    )(s0, col(gg), col(gk), col(gq), gv.reshape(db, GLA_HEADS, 1, GLA_DV), gr.reshape(db, GLA_HEADS, GLA_DV),
      g_gla_norm.reshape(1, 1, GLA_DV))


def _paged_unit(unit, n_units, units_per_seq, pt_ref, q_ref, kn_ref, vn_ref, lam_refs, gdn_ref, kc_ref, vc_ref,
                o_ref, kbuf, vbuf, sem, qm_scr, m_scr, l_scr, acc_scr, pages):
    lq1_ref, lk1_ref, lq2_ref, lk2_ref = lam_refs
    seq = unit // units_per_seq
    part = unit % units_per_seq
    slot = unit % 2
    width = DIFF_HEADS * 2 * DIFF_HD
    n_maps = 2 * DIFF_HEADS

    def page_copies(u, slot_idx):
        b = u // units_per_seq
        first = (u % units_per_seq) * pages
        out = []
        for i in range(pages):
            page = pt_ref[b, first + i]
            out.append(pltpu.make_async_copy(kc_ref.at[page], kbuf.at[slot_idx, i], sem.at[0, slot_idx]))
            out.append(pltpu.make_async_copy(vc_ref.at[page], vbuf.at[slot_idx, i], sem.at[1, slot_idx]))
        return out

    @pl.when(unit == 0)
    def _():
        for cp in page_copies(unit, slot):
            cp.start()

    @pl.when(unit + 1 < n_units)
    def _():
        for cp in page_copies(unit + 1, 1 - slot):
            cp.start()

    for cp in page_copies(unit, slot):
        cp.wait()
    k_refs = [kbuf.at[slot, i] for i in range(pages)]
    v_refs = [vbuf.at[slot, i] for i in range(pages)]

    @pl.when(part == 0)
    def _():
        row = lax.broadcasted_iota(jnp.int32, (n_maps, width), 0)
        lane = lax.broadcasted_iota(jnp.int32, (n_maps, width), 1)
        q_rows = jnp.where(lane // DIFF_HD == row, jnp.broadcast_to(q_ref[seq], (n_maps, width)), 0.0)
        qm_scr[...] = q_rows
        m_scr[...] = jnp.sum(q_rows * kn_ref[seq], axis=-1, keepdims=True)
        l_scr[...] = jnp.ones_like(l_scr)
        acc_scr[...] = jnp.broadcast_to(vn_ref[seq], acc_scr.shape)

    qm = qm_scr[...].astype(BF16)
    s = jnp.concatenate([_dot(qm, k_refs[i][...].astype(BF16)) for i in range(pages)], axis=1)
    m_prev = m_scr[...]
    m_new = jnp.maximum(m_prev, jnp.max(s, axis=-1, keepdims=True))
    alpha = jnp.exp2(m_prev - m_new)
    p = jnp.exp2(s - m_new)
    l_scr[...] = alpha * l_scr[...] + jnp.sum(p, axis=-1, keepdims=True)
    acc = alpha * acc_scr[...]
    for i in range(pages):
        p_i = p[:, i * PAGE_SIZE:(i + 1) * PAGE_SIZE].astype(BF16)
        acc = acc + jnp.concatenate(
            [_dot(p_i, v_refs[i][pl.ds(h, PAGE_SIZE, stride=DIFF_HEADS), :].astype(BF16))
             for h in range(DIFF_HEADS)], axis=1)
    acc_scr[...] = acc
    m_scr[...] = m_new

    @pl.when(part == units_per_seq - 1)
    def _():
        o = acc_scr[...] / l_scr[...]
        for h in range(DIFF_HEADS):
            sl = slice(h * DIFF_VD, (h + 1) * DIFF_VD)
            od = _diff_combine(o[2 * h:2 * h + 1, sl], o[2 * h + 1:2 * h + 2, sl],
                               lq1_ref, lk1_ref, lq2_ref, lk2_ref, gdn_ref)
            o_ref[seq, :, sl] = od.astype(BF16)


def _cross_sample_kernel(q_ref, mk_ref, mv_ref, o_ref, *, n_mem, seqs):
    rows = n_mem * X_HEADS
    row = lax.broadcasted_iota(jnp.int32, (8, rows), 0)
    lane = lax.broadcasted_iota(jnp.int32, (8, rows), 1)
    own_head = lane % X_HEADS == row % X_HEADS
    for t in range(seqs):
        q = q_ref[t].astype(F32)
        q_rows = jnp.concatenate([q[:, h * X_HD:(h + 1) * X_HD] for h in range(X_HEADS)]
                                 + [jnp.zeros((8 - X_HEADS, X_HD), F32)], axis=0).astype(BF16)
        s = _dot_nt(q_rows, mk_ref[t].astype(BF16)) * (X_HD ** -0.5)
        s = jnp.where(own_head, s, -jnp.inf)
        p = jnp.exp(s - jnp.max(s, axis=-1, keepdims=True))
        pr = p / jnp.sum(p, axis=-1, keepdims=True)
        o = _dot(pr.astype(BF16), mv_ref[t].astype(BF16))
        for h in range(X_HEADS):
            o_ref[t, :, h * X_HD:(h + 1) * X_HD] = o[h:h + 1, :].astype(BF16)


def _cross_sample(xq, mem_k, mem_v):
    db, rows, _ = mem_k.shape
    width = X_HEADS * X_HD
    seqs = math.gcd(db, 4)
    row = pl.BlockSpec((seqs, 1, width), lambda b: (b, 0, 0))
    mem = pl.BlockSpec((seqs, rows, X_HD), lambda b: (b, 0, 0))
    return pl.pallas_call(
        functools.partial(_cross_sample_kernel, n_mem=rows // X_HEADS, seqs=seqs),
        grid=(db // seqs,),
        in_specs=[row, mem, mem],
        out_specs=row,
        out_shape=jax.ShapeDtypeStruct((db, 1, width), BF16),
        compiler_params=_params("parallel"),
    )(xq.reshape(db, 1, width), mem_k, mem_v)


def _rope_tables(pos):
    half = DIFF_HD // 2
    inv = jnp.exp(-math.log(ROPE_THETA) * jnp.arange(half, dtype=F32) * (2.0 / DIFF_HD))
    ang = pos[:, None] * inv[None, :]
    cos, sin = jnp.cos(ang), jnp.sin(ang)
    cos_l = jnp.concatenate([cos, cos, cos, cos], axis=1)
    sin_l = jnp.concatenate([-sin, sin, -sin, sin], axis=1)
    return cos_l, sin_l, cos.T, sin.T


def _forward(x_prompt, x_sample, cache_k_diff, cache_v_diff, state_gla, cache_mem_k, cache_mem_v, page_table,
             mem_prompt, g_norm1, w_in, w_gla_a2, b_gla_a, g_gla_norm, lambda_q1, lambda_k1, lambda_q2,
             lambda_k2, g_diff_norm, w_br_a, w_br_b, w_gate, b_gate, w_o, g_norm2, g_mem, w_xq, w_xk, w_xv,
             w_xo, g_norm3, w_router, b_router, w_gate_up, b_gate_up, w_down, b_down, g_final):
    b_p, t_p, d = x_prompt.shape
    db, t_s, _ = x_sample.shape
    assert t_s == 1 and d == D_MODEL and w_in.shape[0] == 1
    n_pages = page_table.shape[1]
    n_pool = cache_k_diff.shape[1]
    n_mem = mem_prompt.shape[1]
    row = lambda a: a.reshape(1, -1).astype(F32)

    wi = w_in[0]
    w_main = jnp.concatenate([wi[:, :1536], wi[:, 1552:2064], wi[:, 2576:]], axis=1).astype(BF16)
    w_kt = wi[:, 2064:2576].T.astype(BF16)
    w_ga = jnp.pad(wi[:, 1536:1552], ((0, 0), (0, LANES - GLA_GATE_RANK))).astype(BF16)
    w_a2 = jnp.pad(w_gla_a2[0], ((0, LANES - GLA_GATE_RANK), (0, 0))).astype(BF16)
    g1, g2, g3, gm, gf = row(g_norm1[0]), row(g_norm2[0]), row(g_norm3[0]), row(g_mem[0]), row(g_final)
    lam_params = [row(lambda_q1[0]), row(lambda_k1[0]), row(lambda_q2[0]), row(lambda_k2[0])]
    gdn, ggn = row(g_diff_norm[0]), row(g_gla_norm[0])
    merge_w = [g1, w_gate[0].astype(BF16), row(b_gate[0]), w_br_a[0].astype(BF16), w_br_b[0].astype(BF16),
               w_o[0].astype(BF16), g2, w_xq[0].astype(BF16)]
    wr = jnp.pad(w_router[0], ((0, 0), (0, LANES - N_EXPERTS)))
    wr_hi = wr.astype(BF16)
    wr_lo = (wr - wr_hi.astype(F32)).astype(BF16)
    post_w = [w_xo[0].astype(BF16), g3, wr_hi, wr_lo, row(jnp.pad(b_router[0], (0, LANES - N_EXPERTS)))]
    bgu = b_gate_up[0].reshape(N_EXPERTS, D_FF, 2).transpose(0, 2, 1).reshape(N_EXPERTS, 1, 2 * D_FF)
    moe_w = [w_gate_up[0], bgu.astype(F32), w_down[0], b_down[0].reshape(N_EXPERTS, 1, d)]

    m_p = b_p * t_p
    tm_p = math.gcd(t_p, 512)
    xp = x_prompt.reshape(m_p, d)
    mkf, mvf, mkb, mvb = _memkv(mem_prompt.reshape(b_p * n_mem, d), gm, w_xk[0].astype(BF16),
                                w_xv[0].astype(BF16))
    width = DIFF_HEADS * 2 * DIFF_HD
    gq, gk, gg, gv, gr, dq, dvf, dvb, dkt_f, dkt_b = _proj(
        xp, g1, w_main, w_kt, w_ga, w_a2, row(b_gla_a[0]), _rope_tables(jnp.arange(t_p, dtype=F32)), tm_p, t_p)
    og, gla_p = _gla_prompt(gq, gk, gg, gv, gr, ggn, b_p, t_p)
    k_prompt = dkt_f.reshape(b_p, DIFF_HEADS, 2, DIFF_HD, t_p).transpose(0, 4, 1, 2, 3)

    tm_s = db
    xs = x_sample.reshape(db, d)
    past = n_pages * PAGE_SIZE
    sq, sk, sg, sv, sr, sdq, sdvf, _, sdkt_f, _ = _proj(
        xs, g1, w_main, w_kt, w_ga, w_a2, row(b_gla_a[0]), _rope_tables(jnp.full((db,), past, dtype=F32)), tm_s, db)
    sdkf = sdkt_f[0].T
    gla_s, og_s = _gla_step(state_gla[0], sq, sk, sg, sv, sr, ggn)
    cache_kt = jnp.transpose(cache_k_diff[0], (0, 2, 3, 4, 1)).reshape(n_pool, width, PAGE_SIZE)
    cache_v2 = cache_v_diff[0].reshape(n_pool, PAGE_SIZE * DIFF_HEADS, DIFF_VD)
    od, od_s = _diff_attention(dq, dkt_b, dvb, sdq.astype(F32), sdkf, sdvf, cache_kt, cache_v2, page_table,
                               lam_params, gdn, b_p, t_p)

    x1, xq = _merge(xp, og, od, *merge_w, tm_p)
    x2, h3, ti, tg = _post(x1, xq, *post_w, tm_p, mem=(mkb, mvb), seq=t_p)
    x1s, xqs = _merge(xs, og_s.reshape(db, 512), od_s.reshape(db, 512), *merge_w, tm_s)
    o_s = _cross_sample(xqs, cache_mem_k[0].reshape(db, n_mem * X_HEADS, X_HD),
                        cache_mem_v[0].reshape(db, n_mem * X_HEADS, X_HD))
    x2s, h3s, tis, tgs = _post(x1s, o_s.reshape(db, 512), *post_w, tm_s)
    y_p, y_s = _moe([(x2, h3, ti, tg, math.gcd(m_p, 256)), (x2s, h3s, tis, tgs, db)], moe_w, gf, 256)

    return (y_p.reshape(b_p, t_p, d), y_s.reshape(db, 1, d),
            k_prompt[None], dvf.reshape(1, b_p, t_p, DIFF_HEADS, DIFF_VD),
            gla_p[None], mkf.reshape(1, b_p, n_mem, X_HEADS, X_HD), mvf.reshape(1, b_p, n_mem, X_HEADS, X_HD),
            sdkf.reshape(1, db, 1, DIFF_HEADS, 2, DIFF_HD), sdvf.reshape(1, db, 1, DIFF_HEADS, DIFF_VD),
            gla_s[None])


def kernel(x_prompt, x_sample, cache_k_diff, cache_v_diff, state_gla, cache_mem_k, cache_mem_v, page_table, mem_prompt, g_norm1, w_in, w_gla_a2, b_gla_a, g_gla_norm, lambda_q1, lambda_k1, lambda_q2, lambda_k2, g_diff_norm, w_br_a, w_br_b, w_gate, b_gate, w_o, g_norm2, g_mem, w_xq, w_xk, w_xv, w_xo, g_norm3, w_router, b_router, w_gate_up, b_gate_up, w_down, b_down, g_final):
    return _forward(x_prompt, x_sample, cache_k_diff, cache_v_diff, state_gla, cache_mem_k, cache_mem_v,
                    page_table, mem_prompt, g_norm1, w_in, w_gla_a2, b_gla_a, g_gla_norm, lambda_q1, lambda_k1,
                    lambda_q2, lambda_k2, g_diff_norm, w_br_a, w_br_b, w_gate, b_gate, w_o, g_norm2, g_mem,
                    w_xq, w_xk, w_xv, w_xo, g_norm3, w_router, b_router, w_gate_up, b_gate_up, w_down, b_down,
                    g_final)
```

```python
import functools
import math

import jax
import jax.numpy as jnp
from jax import lax
from jax.experimental import pallas as pl
from jax.experimental.pallas import tpu as pltpu

F32 = jnp.float32
BF16 = jnp.bfloat16

D_MODEL = 1024
GLA_HEADS = 4
GLA_DK = 64
GLA_DV = 128
GLA_GATE_RANK = 16
GLA_GATE_NORM = 16.0
DIFF_HEADS = 4
DIFF_HD = 64
DIFF_VD = 128
X_HEADS = 4
X_HD = 128
N_EXPERTS = 32
TOP_K = 4
D_FF = 1024
SWIGLU_LIMIT = 7.0
SWIGLU_ALPHA = 1.702
PAGE_SIZE = 128
ROPE_THETA = 10000.0
EPS = 1e-6
LAMBDA_INIT = 0.8 - 0.6 * math.exp(-0.3 * 0)

LANES = 128
ROW_TILE = 8
LOG2E = math.log2(math.e)
ATTN_BLOCK = 512
PAGES_PER_UNIT = 16
GLA_CHUNK = 64
GLA_SUB = 16
GLA_EXP_CLAMP = 80.0
VMEM_LIMIT = 56 * 1024 * 1024


def _dot(a, b):
    return jnp.dot(a, b, preferred_element_type=F32)


def _dot_nt(a, b):
    return lax.dot_general(a, b, (((1,), (1,)), ((), ())), preferred_element_type=F32)


def _dot_tn(a, b):
    return lax.dot_general(a, b, (((0,), (0,)), ((), ())), preferred_element_type=F32)


def _rms(x, g):
    return x * lax.rsqrt(jnp.mean(x * x, axis=-1, keepdims=True) + EPS) * g


def _sigmoid(x):
    return 1.0 / (1.0 + jnp.exp(-x))


def _params(*sem):
    return pltpu.CompilerParams(dimension_semantics=sem, vmem_limit_bytes=VMEM_LIMIT)


def _const_spec(shape):
    nd = len(shape)
    return pl.BlockSpec(shape, lambda *_: (0,) * nd)


def _proj_kernel(x_ref, g1_ref, w_ref, wkt_ref, wga_ref, wa2_ref, ba_ref, cos_ref, sin_ref, cost_ref, sint_ref,
                 gq_ref, gk_ref, gg_ref, gv_ref, gr_ref, dq_ref, dvf_ref, dvb_ref, dkf_ref, dkb_ref):
    x = x_ref[...]
    u = _rms(x, g1_ref[...]).astype(BF16)

    def seg(lo, hi):
        return _dot(u, w_ref[:, lo:hi])

    gq_ref[...] = seg(0, 256) * (GLA_DK ** -0.5)
    gk_ref[...] = seg(256, 512)
    gv_ref[...] = seg(512, 1024).astype(BF16)
    gr_ref[...] = seg(1024, 1536).astype(BF16)
    ga = _dot(u, wga_ref[...])
    z = _dot(ga.astype(BF16), wa2_ref[...]) + ba_ref[...]
    log_sig = jnp.minimum(z, 0.0) - jnp.log(1.0 + jnp.exp(-jnp.abs(z)))
    gg_ref[...] = log_sig * (1.0 / GLA_GATE_NORM)

    cos = jnp.concatenate([cos_ref[...]] * 4, axis=1)
    sin = jnp.concatenate([sin_ref[...]] * 4, axis=1)
    width = 4 * LANES
    lane = lax.broadcasted_iota(jnp.int32, (1, width), 1)
    first_half = (lane % DIFF_HD) < (DIFF_HD // 2)

    def rope(v):
        partner = jnp.where(first_half, pltpu.roll(v, width - DIFF_HD // 2, 1), pltpu.roll(v, DIFF_HD // 2, 1))
        return v * cos + partner * sin

    dq = rope(seg(1536, 2048))
    dq_ref[...] = (dq * (DIFF_HD ** -0.5 * LOG2E)).astype(BF16)
    dv = seg(2048, 2560)
    dvf_ref[...] = dv
    dvb_ref[...] = dv.astype(BF16)
    k_t = _dot_nt(wkt_ref[...], u)
    cos_t = cost_ref[...]
    sin_t = sint_ref[...]
    half = DIFF_HD // 2
    slabs = []
    for grp in range(2 * DIFF_HEADS):
        x1 = k_t[grp * DIFF_HD:grp * DIFF_HD + half, :]
        x2 = k_t[grp * DIFF_HD + half:(grp + 1) * DIFF_HD, :]
        slabs += [x1 * cos_t - x2 * sin_t, x2 * cos_t + x1 * sin_t]
    k_rot = jnp.concatenate(slabs, axis=0)
    dkf_ref[0] = k_rot
    dkb_ref[0] = k_rot.astype(BF16)


def _proj(x, g1, w_main, w_kt, w_ga, w_a2, b_a, rope_tables, tm, seq):
    m = x.shape[0]
    cos, sin, cos_t, sin_t = rope_tables
    n_seq = seq // tm
    width = DIFF_HEADS * 2 * DIFF_HD
    tok = lambda w: pl.BlockSpec((tm, w), lambda i: (i, 0))
    rope = pl.BlockSpec((tm, LANES), lambda i: (i % n_seq, 0))
    rope_t = pl.BlockSpec((DIFF_HD // 2, tm), lambda i: (0, i % n_seq))
    key_t = pl.BlockSpec((1, width, tm), lambda i: (i // n_seq, 0, i % n_seq))
    outs = [(256, F32), (256, F32), (256, F32), (512, BF16), (512, BF16), (512, BF16), (512, F32), (512, BF16)]
    return pl.pallas_call(
        _proj_kernel,
        grid=(m // tm,),
        in_specs=[tok(D_MODEL), _const_spec((1, D_MODEL)), _const_spec(w_main.shape), _const_spec(w_kt.shape),
                  _const_spec(w_ga.shape), _const_spec(w_a2.shape), _const_spec((1, 256)),
                  rope, rope, rope_t, rope_t],
        out_specs=[tok(w) for w, _ in outs] + [key_t, key_t],
        out_shape=[jax.ShapeDtypeStruct((m, w), dt) for w, dt in outs]
        + [jax.ShapeDtypeStruct((m // seq, width, seq), F32), jax.ShapeDtypeStruct((m // seq, width, seq), BF16)],
        compiler_params=_params("parallel"),
    )(x, g1, w_main, w_kt, w_ga, w_a2, b_a, cos, sin, cos_t, sin_t)


def _gla_kernel(q_ref, k_ref, g_ref, v_ref, r_ref, gn_ref, o_ref, s_ref, st_scr, *, chunk, sub, seqs):
    c = pl.program_id(1)

    @pl.when(c == 0)
    def _():
        st_scr[...] = jnp.zeros_like(st_scr)

    for sq in range(seqs):
        _gla_chunk(sq, q_ref, k_ref, g_ref, v_ref, r_ref, gn_ref, o_ref, st_scr, chunk, sub)

    @pl.when(c == pl.num_programs(1) - 1)
    def _():
        for sq in range(seqs):
            for pair in range(GLA_HEADS // 2):
                s_ref[sq, 2 * pair:2 * pair + 2] = st_scr[sq, pair].T.reshape(2, GLA_DK, GLA_DV)


def _gla_chunk(sq, q_ref, k_ref, g_ref, v_ref, r_ref, gn_ref, o_ref, st_scr, chunk, sub):
    g = g_ref[sq]
    row = lax.broadcasted_iota(jnp.int32, (chunk, chunk), 0)
    col = lax.broadcasted_iota(jnp.int32, (chunk, chunk), 1)
    causal = col <= row
    tril = jnp.where(causal, 1.0, 0.0).astype(BF16)
    g_hi = g.astype(BF16)
    r1 = g - g_hi.astype(F32)
    g_mid = r1.astype(BF16)
    g_lo = (r1 - g_mid.astype(F32)).astype(BF16)
    cum = _dot(tril, g_hi) + _dot(tril, g_mid) + _dot(tril, g_lo)
    q = q_ref[sq]
    k = k_ref[sq]
    last = cum[chunk - 1:chunk, :]
    q_dec = q * jnp.exp(cum)
    k_dec = k * jnp.exp(last - cum)
    lane = lax.broadcasted_iota(jnp.int32, (1, LANES), 1)
    head_mask = (lane < GLA_DK, lane >= GLA_DK)
    gn = gn_ref[...]

    for pair in range(GLA_HEADS // 2):
        sl = slice(pair * LANES, (pair + 1) * LANES)
        state_t = st_scr[sq, pair]
        cum_p, q_p, k_p = cum[:, sl], q[:, sl], k[:, sl]
        q_dec_p, k_dec_p = q_dec[:, sl], k_dec[:, sl]
        k_anchor = []
        for i in range(chunk // sub):
            anchor = cum_p[i * sub:i * sub + 1, :]
            k_anchor.append((k_p * jnp.exp(jnp.minimum(anchor - cum_p, GLA_EXP_CLAMP))).astype(BF16))
        state_new = jnp.exp(last[:, sl]) * state_t
        state_bf = state_t.astype(BF16)
        for hh in range(2):
            h = 2 * pair + hh
            hm = head_mask[hh]
            v_h = v_ref[sq, :, h * GLA_DV:(h + 1) * GLA_DV]
            rows = []
            for i in range(chunk // sub):
                r0 = i * sub
                q_a = q_p[r0:r0 + sub, :] * jnp.exp(cum_p[r0:r0 + sub, :] - cum_p[r0:r0 + 1, :])
                rows.append(_dot_nt(jnp.where(hm, q_a, 0.0).astype(BF16), k_anchor[i]))
            a = jnp.where(causal, jnp.concatenate(rows, axis=0), 0.0).astype(BF16)
            o = _dot_nt(jnp.where(hm, q_dec_p, 0.0).astype(BF16), state_bf) + _dot(a, v_h)
            state_new = state_new + _dot_tn(v_h, jnp.where(hm, k_dec_p, 0.0).astype(BF16))
            r = r_ref[sq, :, h * GLA_DV:(h + 1) * GLA_DV].astype(F32)
            o_ref[sq, :, h * GLA_DV:(h + 1) * GLA_DV] = (_rms(o, gn) * (r * _sigmoid(r))).astype(BF16)
        st_scr[sq, pair] = state_new


def _gla_prompt(gq, gk, gg, gv, gr, g_gla_norm, batch, seq):
    chunk = math.gcd(seq, GLA_CHUNK)
    sub = math.gcd(chunk, GLA_SUB)
    seqs = math.gcd(batch, 4)
    tok = lambda w: pl.BlockSpec((seqs, chunk, w), lambda b, c: (b, c, 0))
    per_seq = lambda a: a.reshape(batch, seq, a.shape[-1])
    og, state = pl.pallas_call(
        functools.partial(_gla_kernel, chunk=chunk, sub=sub, seqs=seqs),
        grid=(batch // seqs, seq // chunk),
        in_specs=[tok(256), tok(256), tok(256), tok(512), tok(512), _const_spec((1, GLA_DV))],
        out_specs=[tok(512), pl.BlockSpec((seqs, GLA_HEADS, GLA_DK, GLA_DV), lambda b, c: (b, 0, 0, 0))],
        out_shape=[jax.ShapeDtypeStruct((batch, seq, 512), BF16),
                   jax.ShapeDtypeStruct((batch, GLA_HEADS, GLA_DK, GLA_DV), F32)],
        scratch_shapes=[pltpu.VMEM((seqs, GLA_HEADS // 2, GLA_DV, 2 * GLA_DK), F32)],
        compiler_params=_params("parallel", "arbitrary"),
    )(per_seq(gq), per_seq(gk), per_seq(gg), per_seq(gv), per_seq(gr), g_gla_norm)
    return og.reshape(batch * seq, 512), state


def _diff_combine(o1, o2, lq1_ref, lk1_ref, lq2_ref, lk2_ref, gdn_ref):
    lam = (jnp.exp(jnp.sum(lq1_ref[...] * lk1_ref[...], axis=-1, keepdims=True))
           - jnp.exp(jnp.sum(lq2_ref[...] * lk2_ref[...], axis=-1, keepdims=True)) + LAMBDA_INIT)
    od = o1 - lam * o2
    return _rms(od, gdn_ref[...]) * (1.0 - LAMBDA_INIT)


def _attn_fused_kernel(qi_tab, ki_tab, pt_ref, q_ref, k_ref, v_ref, lq1_ref, lk1_ref, lq2_ref, lk2_ref, gdn_ref,
                       sq_ref, skn_ref, svn_ref, kc_ref, vc_ref, o_ref, os_ref,
                       m_scr, acc_scr, kbuf, vbuf, sem, qm_scr, pm_scr, pl_scr, pacc_scr,
                       *, blk, pages, n_units, units_per_seq, units_per_step):
    step = pl.program_id(2)
    lin = (pl.program_id(0) * pl.num_programs(1) + pl.program_id(1)) * pl.num_programs(2) + step
    lam_refs = (lq1_ref, lk1_ref, lq2_ref, lk2_ref)
    for u in range(units_per_step):
        unit = lin * units_per_step + u

        @pl.when(unit < n_units)
        def _(unit=unit):
            _paged_unit(unit, n_units, units_per_seq, pt_ref, sq_ref, skn_ref, svn_ref, lam_refs, gdn_ref,
                        kc_ref, vc_ref, os_ref, kbuf, vbuf, sem, qm_scr, pm_scr, pl_scr, pacc_scr, pages)

    _attn_tile(qi_tab[step], ki_tab[step], q_ref, k_ref, v_ref, lam_refs, gdn_ref, o_ref, m_scr, acc_scr, blk)


def _attn_tile(qi, ki, q_ref, k_ref, v_ref, lam_refs, gdn_ref, o_ref, m_scr, acc_scr, blk):
    lq1_ref, lk1_ref, lq2_ref, lk2_ref = lam_refs

    @pl.when(ki == 0)
    def _():
        m_scr[...] = jnp.full_like(m_scr, -jnp.inf)
        acc_scr[...] = jnp.zeros_like(acc_scr)

    lane = lax.broadcasted_iota(jnp.int32, (1, LANES), 1)
    map_mask = (lane < DIFF_HD, lane >= DIFF_HD)

    def update(diagonal):
        q = q_ref[...]
        k_t = k_ref[0]
        v = v_ref[...]
        if diagonal:
            row = lax.broadcasted_iota(jnp.int32, (blk, blk), 0)
            col = lax.broadcasted_iota(jnp.int32, (blk, blk), 1)
            keep = col <= row
        v_ext = jnp.concatenate([v, jnp.ones_like(v)], axis=1)
        scores = [_dot(jnp.where(map_mask[j], q, jnp.zeros_like(q)), k_t) for j in range(2)]
        for j in range(2):
            s = scores[j]
            if diagonal:
                s = jnp.where(keep, s, -jnp.inf)
            m_prev = m_scr[j]
            m_new = jnp.maximum(m_prev, jnp.max(s, axis=-1, keepdims=True))
            alpha = jnp.exp2(m_prev - m_new)
            p = jnp.exp2(s - jnp.concatenate([m_new] * (blk // LANES), axis=1)).astype(BF16)
            acc_scr[j] = jnp.concatenate([alpha, alpha], axis=1) * acc_scr[j] + _dot(p, v_ext)
            m_scr[j] = m_new

    @pl.when(ki < qi)
    def _():
        update(False)

    @pl.when(ki == qi)
    def _():
        update(True)
        o1 = acc_scr[0, :, :DIFF_VD] / acc_scr[0, :, DIFF_VD:]
        o2 = acc_scr[1, :, :DIFF_VD] / acc_scr[1, :, DIFF_VD:]
        o_ref[...] = _diff_combine(o1, o2, lq1_ref, lk1_ref, lq2_ref, lk2_ref, gdn_ref).astype(BF16)


def _diff_attention(dq, dkt, dv, sdq, sdk, sdv, cache_kt, cache_v, page_table, lam_params, g_diff_norm, batch, seq):
    blk = math.gcd(seq, ATTN_BLOCK)
    nb = seq // blk
    pairs = [(qi, ki) for qi in range(nb) for ki in range(qi + 1)]
    qi_tab = jnp.array([p[0] for p in pairs], jnp.int32)
    ki_tab = jnp.array([p[1] for p in pairs], jnp.int32)
    db, n_pages = page_table.shape
    pages = math.gcd(n_pages, PAGES_PER_UNIT)
    units_per_seq = n_pages // pages
    n_units = db * units_per_seq
    n_steps = batch * DIFF_HEADS * len(pairs)
    units_per_step = -(-n_units // n_steps)
    width = DIFF_HEADS * 2 * DIFF_HD
    n_maps = 2 * DIFF_HEADS
    q_spec = pl.BlockSpec((blk, LANES), lambda b, h, s, qt, kt, pt: (b * nb + qt[s], h))
    kv_spec = pl.BlockSpec((blk, LANES), lambda b, h, s, qt, kt, pt: (b * nb + kt[s], h))
    kt_spec = pl.BlockSpec((1, LANES, blk), lambda b, h, s, qt, kt, pt: (b, h, kt[s]))
    vec = lambda w: pl.BlockSpec((1, w), lambda b, h, s, qt, kt, pt: (0, 0))
    rows = pl.BlockSpec((db, 1, width), lambda b, h, s, qt, kt, pt: (0, 0, 0))
    hbm = pl.BlockSpec(memory_space=pl.ANY)
    return pl.pallas_call(
        functools.partial(_attn_fused_kernel, blk=blk, pages=pages, n_units=n_units, units_per_seq=units_per_seq,
                          units_per_step=units_per_step),
        grid_spec=pltpu.PrefetchScalarGridSpec(
            num_scalar_prefetch=3,
            grid=(batch, DIFF_HEADS, len(pairs)),
            in_specs=[q_spec, kt_spec, kv_spec, vec(DIFF_HD), vec(DIFF_HD), vec(DIFF_HD), vec(DIFF_HD),
                      vec(DIFF_VD), rows, rows, rows, hbm, hbm],
            out_specs=[q_spec, rows],
            scratch_shapes=[pltpu.VMEM((2, blk, LANES), F32), pltpu.VMEM((2, blk, 2 * DIFF_VD), F32),
                            pltpu.VMEM((2, pages, width, PAGE_SIZE), F32),
                            pltpu.VMEM((2, pages, PAGE_SIZE * DIFF_HEADS, DIFF_VD), F32),
                            pltpu.SemaphoreType.DMA((2, 2)),
                            pltpu.VMEM((n_maps, width), F32), pltpu.VMEM((n_maps, 1), F32),
                            pltpu.VMEM((n_maps, 1), F32), pltpu.VMEM((n_maps, width), F32)]),
        out_shape=[jax.ShapeDtypeStruct((batch * seq, DIFF_HEADS * DIFF_VD), BF16),
                   jax.ShapeDtypeStruct((db, 1, width), BF16)],
        compiler_params=_params("arbitrary", "arbitrary", "arbitrary"),
    )(qi_tab, ki_tab, page_table, dq, dkt, dv, *lam_params, g_diff_norm, sdq.reshape(db, 1, width),
      sdk.reshape(db, 1, width), sdv.reshape(db, 1, width), cache_kt, cache_v)


def _merge_kernel(x_ref, og_ref, od_ref, g1_ref, wgate_ref, bgate_ref, wbra_ref, wbrb_ref, wo_ref, g2_ref,
                  wxq_ref, x1_ref, xq_ref):
    x = x_ref[...]
    u = _rms(x, g1_ref[...]).astype(BF16)
    gates = _sigmoid(_dot(u, wgate_ref[...]) + bgate_ref[...])
    merged = (gates[:, :D_MODEL] * _dot(og_ref[...], wbra_ref[...])
              + gates[:, D_MODEL:] * _dot(od_ref[...], wbrb_ref[...]))
    x1 = x + _dot(merged.astype(BF16), wo_ref[...])
    x1_ref[...] = x1
    xq_ref[...] = _dot(_rms(x1, g2_ref[...]).astype(BF16), wxq_ref[...]).astype(BF16)


def _merge(x, og, od, g1, w_gate, b_gate, w_br_a, w_br_b, w_o, g2, w_xq, tm):
    m = x.shape[0]
    tok = lambda w: pl.BlockSpec((tm, w), lambda i: (i, 0))
    consts = [g1, w_gate, b_gate, w_br_a, w_br_b, w_o, g2, w_xq]
    return pl.pallas_call(
        _merge_kernel,
        grid=(m // tm,),
        in_specs=[tok(D_MODEL), tok(512), tok(512)] + [_const_spec(c.shape) for c in consts],
        out_specs=[tok(D_MODEL), tok(512)],
        out_shape=[jax.ShapeDtypeStruct((m, D_MODEL), F32), jax.ShapeDtypeStruct((m, 512), BF16)],
        compiler_params=_params("parallel"),
    )(x, og, od, *consts)


def _post_kernel(*refs, shared_memory):
    if shared_memory:
        (x1_ref, xq_ref, mk_ref, mv_ref, wxo_ref, g3_ref, wrh_ref, wrl_ref, br_ref,
         x2_ref, h3_ref, ti_ref, tg_ref) = refs
        xq = xq_ref[...]
        heads = []
        for h in range(X_HEADS):
            sl = slice(h * X_HD, (h + 1) * X_HD)
            s = _dot_nt(xq[:, sl], mk_ref[:, sl]) * (X_HD ** -0.5)
            p = jnp.exp(s - jnp.max(s, axis=-1, keepdims=True))
            pr = p / jnp.sum(p, axis=-1, keepdims=True)
            heads.append(_dot(pr.astype(BF16), mv_ref[:, sl]).astype(BF16))
        o = jnp.concatenate(heads, axis=1)
    else:
        (x1_ref, o_ref, wxo_ref, g3_ref, wrh_ref, wrl_ref, br_ref, x2_ref, h3_ref, ti_ref, tg_ref) = refs
        o = o_ref[...]
    x2 = x1_ref[...] + _dot(o, wxo_ref[...])
    x2_ref[...] = x2
    h3 = _rms(x2, g3_ref[...])
    _to_row_tiles(h3_ref, h3, x2.shape[0])
    h_hi = h3.astype(BF16)
    h_lo = (h3 - h_hi.astype(F32)).astype(BF16)
    logits = (_dot(h_hi, wrh_ref[...]) + _dot(h_lo, wrh_ref[...]) + _dot(h_hi, wrl_ref[...])) + br_ref[...]
    lane = lax.broadcasted_iota(jnp.int32, logits.shape, 1)
    logits = jnp.where(lane < N_EXPERTS, logits, -jnp.inf)
    vals, idxs = [], []
    for _ in range(TOP_K):
        top = jnp.max(logits, axis=-1, keepdims=True)
        idx = jnp.min(jnp.where(logits == top, lane, LANES), axis=-1, keepdims=True)
        vals.append(top)
        idxs.append(idx)
        logits = jnp.where(lane == idx, -jnp.inf, logits)
    exps = [jnp.exp(v - vals[0]) for v in vals]
    denom = exps[0] + exps[1] + exps[2] + exps[3]
    ti = jnp.zeros(lane.shape, jnp.int32)
    tg = jnp.zeros(lane.shape, F32)
    for r in range(TOP_K):
        ti = jnp.where(lane == r, idxs[r], ti)
        tg = jnp.where(lane == r, exps[r] / denom, tg)
    ti_ref[...] = ti
    tg_ref[...] = tg


def _post(x1, attn_in, w_xo, g3, wr_hi, wr_lo, b_r, tm, mem=None, seq=None):
    m = x1.shape[0]
    tok = lambda w: pl.BlockSpec((tm, w), lambda i: (i, 0))
    consts = [w_xo, g3, wr_hi, wr_lo, b_r]
    if mem is not None:
        mk, mv = mem
        n_mem = mk.shape[0] // (m // seq)
        tiles_per_seq = seq // tm
        mem_spec = pl.BlockSpec((n_mem, 512), lambda i: (i // tiles_per_seq, 0))
        in_specs = [tok(D_MODEL), tok(512), mem_spec, mem_spec]
        args = [x1, attn_in, mk, mv]
    else:
        in_specs = [tok(D_MODEL), tok(512)]
        args = [x1, attn_in]
    return pl.pallas_call(
        functools.partial(_post_kernel, shared_memory=mem is not None),
        grid=(m // tm,),
        in_specs=in_specs + [_const_spec(c.shape) for c in consts],
        out_specs=[tok(D_MODEL), pl.BlockSpec((tm * ROW_TILE, LANES), lambda i: (i, 0)), tok(LANES), tok(LANES)],
        out_shape=[jax.ShapeDtypeStruct((m, D_MODEL), F32), jax.ShapeDtypeStruct((m * ROW_TILE, LANES), F32),
                   jax.ShapeDtypeStruct((m, LANES), jnp.int32), jax.ShapeDtypeStruct((m, LANES), F32)],
        compiler_params=_params("parallel"),
    )(*args, *consts)


def _memkv_kernel(m_ref, g_ref, wk_ref, wv_ref, kf_ref, vf_ref, kb_ref, vb_ref):
    u = _rms(m_ref[...], g_ref[...]).astype(BF16)
    mk = _dot(u, wk_ref[...])
    mv = _dot(u, wv_ref[...])
    kf_ref[...] = mk
    vf_ref[...] = mv
    kb_ref[...] = mk.astype(BF16)
    vb_ref[...] = mv.astype(BF16)


def _memkv(mem, g_mem, w_xk, w_xv):
    m = mem.shape[0]
    tm = math.gcd(m, 256)
    tok = lambda w: pl.BlockSpec((tm, w), lambda i: (i, 0))
    return pl.pallas_call(
        _memkv_kernel,
        grid=(m // tm,),
        in_specs=[tok(D_MODEL), _const_spec((1, D_MODEL)), _const_spec(w_xk.shape), _const_spec(w_xv.shape)],
        out_specs=[tok(512)] * 4,
        out_shape=[jax.ShapeDtypeStruct((m, 512), F32)] * 2 + [jax.ShapeDtypeStruct((m, 512), BF16)] * 2,
        compiler_params=_params("parallel"),
    )(mem, g_mem, w_xk, w_xv)


def _row_copy(src_ref, src_row, dst_ref, dst_row, sem):
    src = src_ref.at[pl.ds(pl.multiple_of(src_row * ROW_TILE, ROW_TILE), ROW_TILE), :]
    dst = dst_ref.at[pl.ds(pl.multiple_of(dst_row * ROW_TILE, ROW_TILE), ROW_TILE), :]
    return pltpu.make_async_copy(src, dst, sem)


def _to_row_tiles(ref, value, rows):
    for c in range(D_MODEL // LANES):
        ref[pl.ds(c, rows, stride=ROW_TILE), :] = value[:, c * LANES:(c + 1) * LANES]


def _from_row_tiles(ref, first_row, rows):
    return jnp.concatenate([ref[pl.ds(first_row * ROW_TILE + c, rows, stride=ROW_TILE), :]
                            for c in range(D_MODEL // LANES)], axis=1)


def _start_row_copies(tokens, copies):
    def issue(n, carry):
        for r, cp in enumerate(copies(n)):
            cp.start(priority=r % 2)
        return carry

    lax.fori_loop(0, tokens, issue, 0, unroll=math.gcd(tokens, 8))


def _wait_row_copies(tokens, copies):
    def drain(n, carry):
        for cp in copies(n):
            cp.wait()
        return carry

    lax.fori_loop(0, tokens, drain, 0, unroll=math.gcd(tokens, 8))


def _run_row_copies(tokens, copies):
    _start_row_copies(tokens, copies)
    _wait_row_copies(tokens, copies)


def _dispatch_kernel(slot_ref, h_ref, xs_in_ref, xs_ref, sem, *, tokens):
    del xs_in_ref

    def copies(n):
        return [_row_copy(h_ref, n, xs_ref, slot_ref[0, 0, n * TOP_K + r], sem) for r in range(TOP_K)]

    _run_row_copies(tokens, copies)


def _dispatch(slot, h_tiles, xs, td):
    m = slot.shape[0]
    return pl.pallas_call(
        functools.partial(_dispatch_kernel, tokens=td),
        grid=(m // td,),
        in_specs=[pl.BlockSpec((1, 1, td * TOP_K), lambda i: (i, 0, 0), memory_space=pltpu.SMEM),
                  pl.BlockSpec((td * ROW_TILE, LANES), lambda i: (i, 0)),
                  pl.BlockSpec(memory_space=pl.ANY)],
        out_specs=pl.BlockSpec(memory_space=pl.ANY),
        out_shape=jax.ShapeDtypeStruct(xs.shape, xs.dtype),
        scratch_shapes=[pltpu.SemaphoreType.DMA(())],
        input_output_aliases={2: 0},
        compiler_params=_params("arbitrary"),
    )(slot.reshape(m // td, 1, td * TOP_K), h_tiles, xs)


def _expert_kernel(blk_e, n_used, next_blk, x_ref, wgu_hbm, bgu_ref, wdn_hbm, bdn_ref, y_ref,
                   wgu_f32, wdn_f32, wgu_scr, wdn_scr, sem, count_ref, *, bm):
    i = pl.program_id(0)
    used = i < n_used[0]
    e = blk_e[i]
    new_expert = jnp.logical_or(i == 0, e != blk_e[jnp.maximum(i - 1, 0)])

    def weight_copies(expert, slot):
        return [pltpu.make_async_copy(wgu_hbm.at[expert], wgu_f32.at[slot], sem.at[0, slot]),
                pltpu.make_async_copy(wdn_hbm.at[expert], wdn_f32.at[slot], sem.at[1, slot])]

    @pl.when(i == 0)
    def _():
        count_ref[0] = 0

    @pl.when(jnp.logical_and(used, new_expert))
    def _():
        slot = count_ref[0] % 2
        count_ref[0] = count_ref[0] + 1

        @pl.when(i == 0)
        def _():
            for cp in weight_copies(e, slot):
                cp.start()

        following = next_blk[e]

        @pl.when(following < n_used[0])
        def _():
            for cp in weight_copies(blk_e[jnp.minimum(following, pl.num_programs(0) - 1)], 1 - slot):
                cp.start()

        for cp in weight_copies(e, slot):
            cp.wait()
        wdn_scr[...] = wdn_f32[slot].astype(BF16)
        group = 2 * LANES
        r = lax.broadcasted_iota(jnp.int32, (group, group), 0)
        c = lax.broadcasted_iota(jnp.int32, (group, group), 1)
        perm = jnp.where(r == 2 * (c % LANES) + c // LANES, 1.0, 0.0).astype(BF16)
        for g in range(2 * D_FF // group):
            w = _dot(wgu_f32[slot, :, g * group:(g + 1) * group].astype(BF16), perm).astype(BF16)
            wgu_scr[:, g * LANES:(g + 1) * LANES] = w[:, :LANES]
            wgu_scr[:, D_FF + g * LANES:D_FF + (g + 1) * LANES] = w[:, LANES:]

    @pl.when(used)
    def _():
        x = _from_row_tiles(x_ref, 0, bm).astype(BF16)
        h = _dot(x, wgu_scr[...]) + bgu_ref[0]
        glu = jnp.minimum(h[:, :D_FF], SWIGLU_LIMIT)
        lin = jnp.clip(h[:, D_FF:], -SWIGLU_LIMIT, SWIGLU_LIMIT)
        act = (lin + 1.0) * glu * _sigmoid(SWIGLU_ALPHA * glu)
        _to_row_tiles(y_ref, _dot(act.astype(BF16), wdn_scr[...]) + bdn_ref[0], bm)

    @pl.when(jnp.logical_not(used))
    def _():
        y_ref[...] = jnp.zeros_like(y_ref)


def _experts(xs, blk_e, n_used, next_blk, wgu, bgu, wdn, bdn, bm):
    n_blk = xs.shape[0] // (bm * ROW_TILE)
    rows = pl.BlockSpec((bm * ROW_TILE, LANES), lambda i, be, nu, nb: (i, 0))
    hbm = pl.BlockSpec(memory_space=pl.ANY)
    return pl.pallas_call(
        functools.partial(_expert_kernel, bm=bm),
        grid_spec=pltpu.PrefetchScalarGridSpec(
            num_scalar_prefetch=3,
            grid=(n_blk,),
            in_specs=[rows, hbm,
                      pl.BlockSpec((1, 1, 2 * D_FF), lambda i, be, nu, nb: (be[i], 0, 0)), hbm,
                      pl.BlockSpec((1, 1, D_MODEL), lambda i, be, nu, nb: (be[i], 0, 0))],
            out_specs=rows,
            scratch_shapes=[pltpu.VMEM((2, D_MODEL, 2 * D_FF), F32), pltpu.VMEM((2, D_FF, D_MODEL), F32),
                            pltpu.VMEM((D_MODEL, 2 * D_FF), BF16), pltpu.VMEM((D_FF, D_MODEL), BF16),
                            pltpu.SemaphoreType.DMA((2, 2)), pltpu.SMEM((1,), jnp.int32)]),
        out_shape=jax.ShapeDtypeStruct(xs.shape, F32),
        compiler_params=_params("arbitrary"),
    )(blk_e, n_used, next_blk, xs, wgu, bgu, wdn, bdn)


def _combine_kernel(slot_ref, next_slot_ref, x2_ref, tg_ref, gf_ref, ys_ref, o_ref, buf, sem, *, tokens):
    i = pl.program_id(0)
    cur = i % 2

    def gather(rows_ref, buf_slot):
        def copies(n):
            return [_row_copy(ys_ref, rows_ref[0, 0, n * TOP_K + r], buf.at[buf_slot], r * tokens + n,
                              sem.at[buf_slot]) for r in range(TOP_K)]
        return copies

    @pl.when(i == 0)
    def _():
        _start_row_copies(tokens, gather(slot_ref, cur))

    @pl.when(i + 1 < pl.num_programs(0))
    def _():
        _start_row_copies(tokens, gather(next_slot_ref, 1 - cur))

    _wait_row_copies(tokens, gather(slot_ref, cur))
    tg = tg_ref[...]
    acc = x2_ref[...]
    for r in range(TOP_K):
        acc = acc + tg[:, r:r + 1] * _from_row_tiles(buf.at[cur], r * tokens, tokens)
    o_ref[...] = _rms(acc, gf_ref[...])


def _combine(slot, x2, tg, g_final, ys, tc):
    m = x2.shape[0]
    n_tiles = m // tc
    tok = lambda w: pl.BlockSpec((tc, w), lambda i: (i, 0))
    tile_slots = slot.reshape(n_tiles, 1, tc * TOP_K)
    return pl.pallas_call(
        functools.partial(_combine_kernel, tokens=tc),
        grid=(n_tiles,),
        in_specs=[pl.BlockSpec((1, 1, tc * TOP_K), lambda i: (i, 0, 0), memory_space=pltpu.SMEM),
                  pl.BlockSpec((1, 1, tc * TOP_K), lambda i: (jnp.minimum(i + 1, n_tiles - 1), 0, 0),
                               memory_space=pltpu.SMEM),
                  tok(D_MODEL), tok(LANES), _const_spec((1, D_MODEL)), pl.BlockSpec(memory_space=pl.ANY)],
        out_specs=tok(D_MODEL),
        out_shape=jax.ShapeDtypeStruct((m, D_MODEL), F32),
        scratch_shapes=[pltpu.VMEM((2, TOP_K * tc * ROW_TILE, LANES), F32), pltpu.SemaphoreType.DMA((2,))],
        compiler_params=_params("arbitrary"),
    )(tile_slots, tile_slots, x2, tg, g_final, ys)


def _moe(groups, moe_w, g_final, bm):
    wgu, bgu, wdn, bdn = moe_w
    e = jnp.concatenate([g[2][:, :TOP_K] for g in groups], axis=0)
    m = e.shape[0]
    n_asg = m * TOP_K
    onehot = (e[:, :, None] == jnp.arange(N_EXPERTS, dtype=jnp.int32)).astype(jnp.int32)
    per_tok = jnp.sum(onehot, axis=1)
    before = jnp.cumsum(per_tok, axis=0) - per_tok
    counts = jnp.sum(per_tok, axis=0)
    padded = (counts + bm - 1) // bm * bm
    pad_end = jnp.cumsum(padded)
    pad_start = pad_end - padded
    slot = jnp.sum(onehot * (before + pad_start)[:, None, :], axis=-1).astype(jnp.int32)
    n_blk = -(-n_asg // bm) + N_EXPERTS
    blk_start = jnp.arange(n_blk, dtype=jnp.int32) * bm
    blk_e = jnp.minimum(jnp.sum((blk_start[:, None] >= pad_end[None, :]).astype(jnp.int32), axis=1),
                        N_EXPERTS - 1).astype(jnp.int32)
    n_used = (pad_end[-1:] // bm).astype(jnp.int32)
    xs = jnp.zeros((n_blk * bm * ROW_TILE, LANES), F32)
    slots, start = [], 0
    for x2, h_tiles, _, _, tile in groups:
        slots.append(slot[start:start + x2.shape[0]])
        start += x2.shape[0]
        xs = _dispatch(slots[-1], h_tiles, xs, tile)
    next_blk = (pad_end // bm).astype(jnp.int32)
    ys = _experts(xs, blk_e, n_used, next_blk, wgu, bgu, wdn, bdn, bm)
    return [_combine(s, x2, tg, g_final, ys, tile) for s, (x2, _, _, tg, tile) in zip(slots, groups)]


def _gla_step_kernel(s0_ref, dcol_ref, kcol_ref, qcol_ref, v_ref, r_ref, gn_ref, s_ref, o_ref):
    s_new = jnp.exp(dcol_ref[...]) * s0_ref[...] + kcol_ref[...] * v_ref[...]
    s_ref[...] = s_new
    o = jnp.sum(qcol_ref[...] * s_new, axis=2)
    r = r_ref[...].astype(F32)
    o_ref[...] = (_rms(o, gn_ref[...]) * (r * _sigmoid(r))).astype(BF16)


def _gla_step(s0, gq, gk, gg, gv, gr, g_gla_norm):
    db = s0.shape[0]
    tb = math.gcd(db, 8)
    col = lambda a: jnp.broadcast_to(a.reshape(db, GLA_HEADS, GLA_DK, 1), (db, GLA_HEADS, GLA_DK, GLA_DV))
    big = pl.BlockSpec((tb, GLA_HEADS, GLA_DK, GLA_DV), lambda i: (i, 0, 0, 0))
    return pl.pallas_call(
        _gla_step_kernel,
        grid=(db // tb,),
        in_specs=[big, big, big, big,
                  pl.BlockSpec((tb, GLA_HEADS, 1, GLA_DV), lambda i: (i, 0, 0, 0)),
                  pl.BlockSpec((tb, GLA_HEADS, GLA_DV), lambda i: (i, 0, 0)),
                  pl.BlockSpec((1, 1, GLA_DV), lambda i: (0, 0, 0))],
        out_specs=[big, pl.BlockSpec((tb, GLA_HEADS, GLA_DV), lambda i: (i, 0, 0))],
        out_shape=[jax.ShapeDtypeStruct(s0.shape, F32), jax.ShapeDtypeStruct((db, GLA_HEADS, GLA_DV), BF16)],
        compiler_params=_params("parallel"),
    )(s0, col(gg), col(gk), col(gq), gv.reshape(db, GLA_HEADS, 1, GLA_DV), gr.reshape(db, GLA_HEADS, GLA_DV),
      g_gla_norm.reshape(1, 1, GLA_DV))


def _paged_unit(unit, n_units, units_per_seq, pt_ref, q_ref, kn_ref, vn_ref, lam_refs, gdn_ref, kc_ref, vc_ref,
                o_ref, kbuf, vbuf, sem, qm_scr, m_scr, l_scr, acc_scr, pages):
    lq1_ref, lk1_ref, lq2_ref, lk2_ref = lam_refs
    seq = unit // units_per_seq
    part = unit % units_per_seq
    slot = unit % 2
    width = DIFF_HEADS * 2 * DIFF_HD
    n_maps = 2 * DIFF_HEADS

    def page_copies(u, slot_idx):
        b = u // units_per_seq
        first = (u % units_per_seq) * pages
        out = []
        for i in range(pages):
            page = pt_ref[b, first + i]
            out.append(pltpu.make_async_copy(kc_ref.at[page], kbuf.at[slot_idx, i], sem.at[0, slot_idx]))
            out.append(pltpu.make_async_copy(vc_ref.at[page], vbuf.at[slot_idx, i], sem.at[1, slot_idx]))
        return out

    @pl.when(unit == 0)
    def _():
        for cp in page_copies(unit, slot):
            cp.start()

    @pl.when(unit + 1 < n_units)
    def _():
        for cp in page_copies(unit + 1, 1 - slot):
            cp.start()

    for cp in page_copies(unit, slot):
        cp.wait()
    k_refs = [kbuf.at[slot, i] for i in range(pages)]
    v_refs = [vbuf.at[slot, i] for i in range(pages)]

    @pl.when(part == 0)
    def _():
        row = lax.broadcasted_iota(jnp.int32, (n_maps, width), 0)
        lane = lax.broadcasted_iota(jnp.int32, (n_maps, width), 1)
        q_rows = jnp.where(lane // DIFF_HD == row, jnp.broadcast_to(q_ref[seq], (n_maps, width)), 0.0)
        qm_scr[...] = q_rows
        m_scr[...] = jnp.sum(q_rows * kn_ref[seq], axis=-1, keepdims=True)
        l_scr[...] = jnp.ones_like(l_scr)
        acc_scr[...] = jnp.broadcast_to(vn_ref[seq], acc_scr.shape)

    qm = qm_scr[...].astype(BF16)
    s = jnp.concatenate([_dot(qm, k_refs[i][...].astype(BF16)) for i in range(pages)], axis=1)
    m_prev = m_scr[...]
    m_new = jnp.maximum(m_prev, jnp.max(s, axis=-1, keepdims=True))
    alpha = jnp.exp2(m_prev - m_new)
    p = jnp.exp2(s - m_new)
    l_scr[...] = alpha * l_scr[...] + jnp.sum(p, axis=-1, keepdims=True)
    acc = alpha * acc_scr[...]
    for i in range(pages):
        p_i = p[:, i * PAGE_SIZE:(i + 1) * PAGE_SIZE].astype(BF16)
        acc = acc + jnp.concatenate(
            [_dot(p_i, v_refs[i][pl.ds(h, PAGE_SIZE, stride=DIFF_HEADS), :].astype(BF16))
             for h in range(DIFF_HEADS)], axis=1)
    acc_scr[...] = acc
    m_scr[...] = m_new

    @pl.when(part == units_per_seq - 1)
    def _():
        o = acc_scr[...] / l_scr[...]
        for h in range(DIFF_HEADS):
            sl = slice(h * DIFF_VD, (h + 1) * DIFF_VD)
            od = _diff_combine(o[2 * h:2 * h + 1, sl], o[2 * h + 1:2 * h + 2, sl],
                               lq1_ref, lk1_ref, lq2_ref, lk2_ref, gdn_ref)
            o_ref[seq, :, sl] = od.astype(BF16)


def _cross_sample_kernel(q_ref, mk_ref, mv_ref, o_ref, *, n_mem, seqs):
    rows = n_mem * X_HEADS
    row = lax.broadcasted_iota(jnp.int32, (8, rows), 0)
    lane = lax.broadcasted_iota(jnp.int32, (8, rows), 1)
    own_head = lane % X_HEADS == row % X_HEADS
    for t in range(seqs):
        q = q_ref[t].astype(F32)
        q_rows = jnp.concatenate([q[:, h * X_HD:(h + 1) * X_HD] for h in range(X_HEADS)]
                                 + [jnp.zeros((8 - X_HEADS, X_HD), F32)], axis=0).astype(BF16)
        s = _dot_nt(q_rows, mk_ref[t].astype(BF16)) * (X_HD ** -0.5)
        s = jnp.where(own_head, s, -jnp.inf)
        p = jnp.exp(s - jnp.max(s, axis=-1, keepdims=True))
        pr = p / jnp.sum(p, axis=-1, keepdims=True)
        o = _dot(pr.astype(BF16), mv_ref[t].astype(BF16))
        for h in range(X_HEADS):
            o_ref[t, :, h * X_HD:(h + 1) * X_HD] = o[h:h + 1, :].astype(BF16)


def _cross_sample(xq, mem_k, mem_v):
    db, rows, _ = mem_k.shape
    width = X_HEADS * X_HD
    seqs = math.gcd(db, 4)
    row = pl.BlockSpec((seqs, 1, width), lambda b: (b, 0, 0))
    mem = pl.BlockSpec((seqs, rows, X_HD), lambda b: (b, 0, 0))
    return pl.pallas_call(
        functools.partial(_cross_sample_kernel, n_mem=rows // X_HEADS, seqs=seqs),
        grid=(db // seqs,),
        in_specs=[row, mem, mem],
        out_specs=row,
        out_shape=jax.ShapeDtypeStruct((db, 1, width), BF16),
        compiler_params=_params("parallel"),
    )(xq.reshape(db, 1, width), mem_k, mem_v)


def _rope_tables(pos):
    half = DIFF_HD // 2
    inv = jnp.exp(-math.log(ROPE_THETA) * jnp.arange(half, dtype=F32) * (2.0 / DIFF_HD))
    ang = pos[:, None] * inv[None, :]
    cos, sin = jnp.cos(ang), jnp.sin(ang)
    cos_l = jnp.concatenate([cos, cos, cos, cos], axis=1)
    sin_l = jnp.concatenate([-sin, sin, -sin, sin], axis=1)
    return cos_l, sin_l, cos.T, sin.T


def _forward(x_prompt, x_sample, cache_k_diff, cache_v_diff, state_gla, cache_mem_k, cache_mem_v, page_table,
             mem_prompt, g_norm1, w_in, w_gla_a2, b_gla_a, g_gla_norm, lambda_q1, lambda_k1, lambda_q2,
             lambda_k2, g_diff_norm, w_br_a, w_br_b, w_gate, b_gate, w_o, g_norm2, g_mem, w_xq, w_xk, w_xv,
             w_xo, g_norm3, w_router, b_router, w_gate_up, b_gate_up, w_down, b_down, g_final):
    b_p, t_p, d = x_prompt.shape
    db, t_s, _ = x_sample.shape
    assert t_s == 1 and d == D_MODEL and w_in.shape[0] == 1
    n_pages = page_table.shape[1]
    n_pool = cache_k_diff.shape[1]
    n_mem = mem_prompt.shape[1]
    row = lambda a: a.reshape(1, -1).astype(F32)

    wi = w_in[0]
    w_main = jnp.concatenate([wi[:, :1536], wi[:, 1552:2064], wi[:, 2576:]], axis=1).astype(BF16)
    w_kt = wi[:, 2064:2576].T.astype(BF16)
    w_ga = jnp.pad(wi[:, 1536:1552], ((0, 0), (0, LANES - GLA_GATE_RANK))).astype(BF16)
    w_a2 = jnp.pad(w_gla_a2[0], ((0, LANES - GLA_GATE_RANK), (0, 0))).astype(BF16)
    g1, g2, g3, gm, gf = row(g_norm1[0]), row(g_norm2[0]), row(g_norm3[0]), row(g_mem[0]), row(g_final)
    lam_params = [row(lambda_q1[0]), row(lambda_k1[0]), row(lambda_q2[0]), row(lambda_k2[0])]
    gdn, ggn = row(g_diff_norm[0]), row(g_gla_norm[0])
    merge_w = [g1, w_gate[0].astype(BF16), row(b_gate[0]), w_br_a[0].astype(BF16), w_br_b[0].astype(BF16),
               w_o[0].astype(BF16), g2, w_xq[0].astype(BF16)]
    wr = jnp.pad(w_router[0], ((0, 0), (0, LANES - N_EXPERTS)))
    wr_hi = wr.astype(BF16)
    wr_lo = (wr - wr_hi.astype(F32)).astype(BF16)
    post_w = [w_xo[0].astype(BF16), g3, wr_hi, wr_lo, row(jnp.pad(b_router[0], (0, LANES - N_EXPERTS)))]
    bgu = b_gate_up[0].reshape(N_EXPERTS, D_FF, 2).transpose(0, 2, 1).reshape(N_EXPERTS, 1, 2 * D_FF)
    moe_w = [w_gate_up[0], bgu.astype(F32), w_down[0], b_down[0].reshape(N_EXPERTS, 1, d)]

    m_p = b_p * t_p
    tm_p = math.gcd(t_p, 512)
    xp = x_prompt.reshape(m_p, d)
    mkf, mvf, mkb, mvb = _memkv(mem_prompt.reshape(b_p * n_mem, d), gm, w_xk[0].astype(BF16),
                                w_xv[0].astype(BF16))
    width = DIFF_HEADS * 2 * DIFF_HD
    gq, gk, gg, gv, gr, dq, dvf, dvb, dkt_f, dkt_b = _proj(
        xp, g1, w_main, w_kt, w_ga, w_a2, row(b_gla_a[0]), _rope_tables(jnp.arange(t_p, dtype=F32)), tm_p, t_p)
    og, gla_p = _gla_prompt(gq, gk, gg, gv, gr, ggn, b_p, t_p)
    k_prompt = dkt_f.reshape(b_p, DIFF_HEADS, 2, DIFF_HD, t_p).transpose(0, 4, 1, 2, 3)

    tm_s = db
    xs = x_sample.reshape(db, d)
    past = n_pages * PAGE_SIZE
    sq, sk, sg, sv, sr, sdq, sdvf, _, sdkt_f, _ = _proj(
        xs, g1, w_main, w_kt, w_ga, w_a2, row(b_gla_a[0]), _rope_tables(jnp.full((db,), past, dtype=F32)), tm_s, db)
    sdkf = sdkt_f[0].T
    gla_s, og_s = _gla_step(state_gla[0], sq, sk, sg, sv, sr, ggn)
    cache_kt = jnp.transpose(cache_k_diff[0], (0, 2, 3, 4, 1)).reshape(n_pool, width, PAGE_SIZE)
    cache_v2 = cache_v_diff[0].reshape(n_pool, PAGE_SIZE * DIFF_HEADS, DIFF_VD)
    od, od_s = _diff_attention(dq, dkt_b, dvb, sdq.astype(F32), sdkf, sdvf, cache_kt, cache_v2, page_table,
                               lam_params, gdn, b_p, t_p)

    x1, xq = _merge(xp, og, od, *merge_w, tm_p)
    x2, h3, ti, tg = _post(x1, xq, *post_w, tm_p, mem=(mkb, mvb), seq=t_p)
    x1s, xqs = _merge(xs, og_s.reshape(db, 512), od_s.reshape(db, 512), *merge_w, tm_s)
    o_s = _cross_sample(xqs, cache_mem_k[0].reshape(db, n_mem * X_HEADS, X_HD),
                        cache_mem_v[0].reshape(db, n_mem * X_HEADS, X_HD))
    x2s, h3s, tis, tgs = _post(x1s, o_s.reshape(db, 512), *post_w, tm_s)
    y_p, y_s = _moe([(x2, h3, ti, tg, math.gcd(m_p, 256)), (x2s, h3s, tis, tgs, db)], moe_w, gf, 256)

    return (y_p.reshape(b_p, t_p, d), y_s.reshape(db, 1, d),
            k_prompt[None], dvf.reshape(1, b_p, t_p, DIFF_HEADS, DIFF_VD),
            gla_p[None], mkf.reshape(1, b_p, n_mem, X_HEADS, X_HD), mvf.reshape(1, b_p, n_mem, X_HEADS, X_HD),
            sdkf.reshape(1, db, 1, DIFF_HEADS, 2, DIFF_HD), sdvf.reshape(1, db, 1, DIFF_HEADS, DIFF_VD),
            gla_s[None])


def kernel(x_prompt, x_sample, cache_k_diff, cache_v_diff, state_gla, cache_mem_k, cache_mem_v, page_table, mem_prompt, g_norm1, w_in, w_gla_a2, b_gla_a, g_gla_norm, lambda_q1, lambda_k1, lambda_q2, lambda_k2, g_diff_norm, w_br_a, w_br_b, w_gate, b_gate, w_o, g_norm2, g_mem, w_xq, w_xk, w_xv, w_xo, g_norm3, w_router, b_router, w_gate_up, b_gate_up, w_down, b_down, g_final):
    return _forward(x_prompt, x_sample, cache_k_diff, cache_v_diff, state_gla, cache_mem_k, cache_mem_v,
                    page_table, mem_prompt, g_norm1, w_in, w_gla_a2, b_gla_a, g_gla_norm, lambda_q1, lambda_k1,
                    lambda_q2, lambda_k2, g_diff_norm, w_br_a, w_br_b, w_gate, b_gate, w_o, g_norm2, g_mem,
                    w_xq, w_xk, w_xv, w_xo, g_norm3, w_router, b_router, w_gate_up, b_gate_up, w_down, b_down,
                    g_final)
```
